```python
import math
import jax, jax.numpy as jnp
from jax import lax
import numpy as np

D_MODEL = 1024
BATCH = 16
SEQ = 4096
DEPTH = 1

MIX_WIDTH = D_MODEL
GLA_WIDTH = MIX_WIDTH // 2
S5_WIDTH = MIX_WIDTH - GLA_WIDTH
GLA_HEADS = 4
GLA_KEY_DIM = GLA_WIDTH // 2
GLA_DK = GLA_KEY_DIM // GLA_HEADS
GLA_DV = GLA_WIDTH // GLA_HEADS
GLA_GATE_RANK = 16
GLA_GATE_NORMALIZER = 16.0
GLA_CHUNK = 64
S5_GROUP = 16
S5_GROUPS = S5_WIDTH // S5_GROUP
S5_STATE = 64
S5_DT_MIN = 1e-3
S5_DT_MAX = 1e-1
D_FF = 4 * D_MODEL
EPS = 1e-6
IN_COLS = 2 * GLA_KEY_DIM + 2 * GLA_WIDTH + GLA_GATE_RANK + S5_WIDTH
IN_SPLITS = (
    GLA_KEY_DIM,
    2 * GLA_KEY_DIM,
    2 * GLA_KEY_DIM + GLA_WIDTH,
    2 * GLA_KEY_DIM + GLA_WIDTH + GLA_GATE_RANK,
    2 * GLA_KEY_DIM + 2 * GLA_WIDTH + GLA_GATE_RANK,
)

kernel_name = "hymba_style_gla_s5_hybrid"


def rmsnorm(x, w):
    xf = x.astype(jnp.float32)
    y = xf * lax.rsqrt(jnp.mean(xf * xf, axis=-1, keepdims=True) + EPS)
    return (y * w.astype(jnp.float32)).astype(x.dtype)


def gla_mixer(q, k, v, gk_lr, g, w_gk_up, b_gk, norm_w):
    f32 = jnp.float32
    bsz, seq, _ = q.shape
    n_chunks = seq // GLA_CHUNK

    def heads(t, d):
        t = t.astype(f32).reshape(bsz, n_chunks, GLA_CHUNK, GLA_HEADS, d)
        return t.transpose(0, 3, 1, 2, 4)

    gk = gk_lr.astype(f32) @ w_gk_up.astype(f32) + b_gk.astype(f32)
    log_a = jax.nn.log_sigmoid(gk) / GLA_GATE_NORMALIZER

    qh = heads(q, GLA_DK) * (GLA_DK ** -0.5)
    kh = heads(k, GLA_DK)
    vh = heads(v, GLA_DV)
    b = jnp.cumsum(heads(log_a, GLA_DK), axis=3)
    b_last = b[:, :, :, -1:, :]

    q_e = qh * jnp.exp(b)
    k_e = kh * jnp.exp(-b)
    k_d = kh * jnp.exp(b_last - b)

    mask = jnp.tril(jnp.ones((GLA_CHUNK, GLA_CHUNK), dtype=bool))
    scores = jnp.einsum('bhcid,bhcjd->bhcij', q_e, k_e)
    scores = jnp.where(mask, scores, 0.0)
    o_intra = jnp.einsum('bhcij,bhcjv->bhciv', scores, vh)

    upd = jnp.einsum('bhcld,bhclv->bhcdv', k_d, vh)
    decay = jnp.exp(b_last[:, :, :, 0, :])

    def step(state, inp):
        dec, u = inp
        return dec[..., None] * state + u, state

    s0 = jnp.zeros((bsz, GLA_HEADS, GLA_DK, GLA_DV), f32)
    _, s_prev = lax.scan(step, s0, (jnp.moveaxis(decay, 2, 0), jnp.moveaxis(upd, 2, 0)))
    s_prev = jnp.moveaxis(s_prev, 0, 2)
    o_inter = jnp.einsum('bhcld,bhcdv->bhclv', q_e, s_prev)

    o = (o_intra + o_inter).transpose(0, 2, 3, 1, 4).reshape(bsz, seq, GLA_HEADS, GLA_DV)
    o = o * lax.rsqrt(jnp.mean(o * o, axis=-1, keepdims=True) + EPS) * norm_w.astype(f32)
    return o.reshape(bsz, seq, GLA_WIDTH) * jax.nn.silu(g.astype(f32))


def _complex_scan_combine(left, right):
    a1r, a1i, b1r, b1i = left
    a2r, a2i, b2r, b2i = right
    ar = a2r * a1r - a2i * a1i
    ai = a2r * a1i + a2i * a1r
    br = a2r * b1r - a2i * b1i + b2r
    bi = a2r * b1i + a2i * b1r + b2i
    return (ar, ai, br, bi)


def s5_mixer(u, a_re, a_im, log_dt, b_re, b_im, c_re, c_im, d_skip, w_glu, b_glu):
    f32 = jnp.float32
    bsz, seq, _ = u.shape
    ar = a_re.astype(f32)
    ai = a_im.astype(f32)
    dt = jnp.exp(log_dt.astype(f32))[:, None]
    mag = jnp.exp(ar * dt)
    abr = mag * jnp.cos(ai * dt)
    abi = mag * jnp.sin(ai * dt)
    den = ar * ar + ai * ai
    nr = abr - 1.0
    fr = (nr * ar + abi * ai) / den
    fi = (abi * ar - nr * ai) / den
    br_ = b_re.astype(f32)
    bi_ = b_im.astype(f32)
    bbr = fr[..., None] * br_ - fi[..., None] * bi_
    bbi = fr[..., None] * bi_ + fi[..., None] * br_

    ut = jnp.swapaxes(u.astype(f32).reshape(bsz, seq, S5_GROUPS, S5_GROUP), 0, 1)
    bu_r = jnp.einsum('sbgp,gnp->sbgn', ut, bbr)
    bu_i = jnp.einsum('sbgp,gnp->sbgn', ut, bbi)
    a_r = jnp.broadcast_to(abr[None, None], (seq, 1, S5_GROUPS, S5_STATE))
    a_i = jnp.broadcast_to(abi[None, None], (seq, 1, S5_GROUPS, S5_STATE))
    _, _, xr, xi = lax.associative_scan(_complex_scan_combine, (a_r, a_i, bu_r, bu_i), axis=0)

    y = (jnp.einsum('sbgn,gpn->sbgp', xr, c_re.astype(f32))
         - jnp.einsum('sbgn,gpn->sbgp', xi, c_im.astype(f32))
         + d_skip.astype(f32) * ut)
    y = jnp.swapaxes(y, 0, 1).reshape(bsz, seq, S5_WIDTH)
    z = jax.nn.gelu(y)
    return z * jax.nn.sigmoid(z @ w_glu.astype(f32) + b_glu.astype(f32))


def setup_inputs(seed: int = 0) -> dict:
    key = jax.random.key(seed)
    ks = jax.random.split(key, 24)
    f32 = jnp.float32
    L = DEPTH
    nrm = lambda k, shape, scale: jax.random.normal(k, shape, f32) * scale
    gain = lambda k, shape: 1.0 + 0.01 * jax.random.normal(k, shape, f32)
    x = jax.random.normal(ks[0], (BATCH, SEQ, D_MODEL), f32)
    norm_mix_w = gain(ks[1], (L, D_MODEL))
    w_in = nrm(ks[2], (L, D_MODEL, IN_COLS), D_MODEL ** -0.5)
    w_gk_up = nrm(ks[3], (L, GLA_GATE_RANK, GLA_KEY_DIM), GLA_GATE_RANK ** -0.5)
    b_gk = nrm(ks[4], (L, GLA_KEY_DIM), 0.01)
    gla_norm_w = gain(ks[5], (L, GLA_DV))
    s5_a_re = -0.5 + nrm(ks[6], (L, S5_GROUPS, S5_STATE), 0.01)
    s5_a_im = (math.pi * jnp.arange(S5_STATE, dtype=f32))[None, None, :] + nrm(ks[7], (L, S5_GROUPS, S5_STATE), 0.01)
    s5_log_dt = jax.random.uniform(ks[8], (L, S5_GROUPS), f32, math.log(S5_DT_MIN), math.log(S5_DT_MAX))
    b_scale = (2.0 * S5_GROUP) ** -0.5
    s5_b_re = nrm(ks[9], (L, S5_GROUPS, S5_STATE, S5_GROUP), b_scale)
    s5_b_im = nrm(ks[10], (L, S5_GROUPS, S5_STATE, S5_GROUP), b_scale)
    c_scale = (2.0 * S5_STATE) ** -0.5
    s5_c_re = nrm(ks[11], (L, S5_GROUPS, S5_GROUP, S5_STATE), c_scale)
    s5_c_im = nrm(ks[12], (L, S5_GROUPS, S5_GROUP, S5_STATE), c_scale)
    s5_d = nrm(ks[13], (L, S5_GROUPS, S5_GROUP), 1.0)
    w_glu = nrm(ks[14], (L, S5_WIDTH, S5_WIDTH), S5_WIDTH ** -0.5)
    b_glu = nrm(ks[15], (L, S5_WIDTH), 0.01)
    w_out = nrm(ks[16], (L, MIX_WIDTH, D_MODEL), MIX_WIDTH ** -0.5)
    norm_mlp_w = gain(ks[17], (L, D_MODEL))
    w_mlp_up = nrm(ks[18], (L, D_MODEL, D_FF), D_MODEL ** -0.5)
    w_mlp_down = nrm(ks[19], (L, D_FF, D_MODEL), D_FF ** -0.5)
    norm_final_w = gain(ks[20], (D_MODEL,))
    return {
        "x": x, "norm_mix_w": norm_mix_w, "w_in": w_in, "w_gk_up": w_gk_up, "b_gk": b_gk,
        "gla_norm_w": gla_norm_w, "s5_a_re": s5_a_re, "s5_a_im": s5_a_im, "s5_log_dt": s5_log_dt,
        "s5_b_re": s5_b_re, "s5_b_im": s5_b_im, "s5_c_re": s5_c_re, "s5_c_im": s5_c_im,
        "s5_d": s5_d, "w_glu": w_glu, "b_glu": b_glu, "w_out": w_out, "norm_mlp_w": norm_mlp_w,
        "w_mlp_up": w_mlp_up, "w_mlp_down": w_mlp_down, "norm_final_w": norm_final_w,
    }


def reference(x, norm_mix_w, w_in, w_gk_up, b_gk, gla_norm_w, s5_a_re, s5_a_im, s5_log_dt,
              s5_b_re, s5_b_im, s5_c_re, s5_c_im, s5_d, w_glu, b_glu, w_out, norm_mlp_w,
              w_mlp_up, w_mlp_down, norm_final_w):
    for l in range(DEPTH):
        h = rmsnorm(x, norm_mix_w[l])
        p = h @ w_in[l]
        q, k, v, gk_lr, g, u = jnp.split(p, IN_SPLITS, axis=-1)
        o_gla = gla_mixer(q, k, v, gk_lr, g, w_gk_up[l], b_gk[l], gla_norm_w[l])
        o_s5 = s5_mixer(u, s5_a_re[l], s5_a_im[l], s5_log_dt[l], s5_b_re[l], s5_b_im[l],
                        s5_c_re[l], s5_c_im[l], s5_d[l], w_glu[l], b_glu[l])
        mix = jnp.concatenate([o_gla, o_s5], axis=-1).astype(x.dtype)
        x = x + mix @ w_out[l]
        h = rmsnorm(x, norm_mlp_w[l])
        x = x + jnp.square(jax.nn.relu(h @ w_mlp_up[l])) @ w_mlp_down[l]
    return rmsnorm(x, norm_final_w)
```

```python
import functools
import math

import jax
import jax.numpy as jnp
import numpy as np
from jax import lax
from jax.experimental import pallas as pl
from jax.experimental.pallas import tpu as pltpu

F32 = jnp.float32
BF16 = jnp.bfloat16

D_MODEL = 1024
GLA_WIDTH = 512
S5_WIDTH = 512
GLA_HEADS = 4
GLA_KEY_DIM = 256
GLA_DK = 64
GLA_DV = 128
GLA_GATE_RANK = 16
GLA_GATE_NORMALIZER = 16.0
GLA_CHUNK = 64
S5_GROUP = 16
S5_GROUPS = 32
S5_STATE = 64
D_FF = 4096
EPS = 1e-6

LANES = 128
MXU_DIM = 256
S5_STATES = S5_GROUPS * S5_STATE
S5_SLABS = 2 * S5_STATES // LANES
VMEM_LIMIT_BYTES = 56 * 1024 * 1024

GLA_BLOCK = 256
S5_BLOCK = 32
S5_PITCH = S5_BLOCK + 8
MLP_ROWS = 512
MLP_FF_CHUNK = 1024


def _rmsnorm(x, w):
    return x * lax.rsqrt(jnp.mean(x * x, axis=-1, keepdims=True) + EPS) * w


def _gla_kernel(x_ref, nw_ref, wg_ref, wup_ref, bgk_ref, gnw_ref, o_ref, st_ref):
    L = GLA_BLOCK
    n_chunks = L // GLA_CHUNK

    @pl.when(pl.program_id(1) == 0)
    def _():
        st_ref[...] = jnp.zeros_like(st_ref)

    h = _rmsnorm(x_ref[0], nw_ref[...])
    p = jnp.dot(h.astype(BF16), wg_ref[...], preferred_element_type=F32)
    q = p[:, 0:256]
    k = p[:, 256:512]
    v = p[:, 512:1024]
    g = p[:, 1024:1536]
    gk_lr = p[:, 1536:1664]

    gk = jnp.dot(gk_lr, wup_ref[...], precision=lax.Precision.HIGHEST,
                 preferred_element_type=F32) + bgk_ref[...]
    log_a = (jnp.minimum(gk, 0.0) - jnp.log1p(jnp.exp(-jnp.abs(gk)))) * (1.0 / GLA_GATE_NORMALIZER)

    row = lax.broadcasted_iota(jnp.int32, (L, L), 0)
    col = lax.broadcasted_iota(jnp.int32, (L, L), 1)
    causal = ((row // GLA_CHUNK) == (col // GLA_CHUNK)) & (col <= row)
    tri = jnp.where(causal, 1.0, 0.0).astype(F32)
    b = jnp.dot(tri, log_a, precision=lax.Precision.HIGHEST, preferred_element_type=F32)
    b_last = jnp.concatenate(
        [jnp.broadcast_to(b[c * GLA_CHUNK + GLA_CHUNK - 1:(c + 1) * GLA_CHUNK, :], (GLA_CHUNK, GLA_KEY_DIM))
         for c in range(n_chunks)], axis=0)

    q_e = (q * (GLA_DK ** -0.5)) * jnp.exp(b)
    k_e = (k * jnp.exp(-b)).astype(BF16)
    k_d = (k * jnp.exp(b_last - b)).astype(BF16)
    v_b = v.astype(BF16)

    lane_head = lax.broadcasted_iota(jnp.int32, (L, GLA_KEY_DIM), 1) // GLA_DK
    o_intra = []
    for hh in range(GLA_HEADS):
        q_h = jnp.where(lane_head == hh, q_e, 0.0).astype(BF16)
        s = lax.dot_general(q_h, k_e, (((1,), (1,)), ((), ())), preferred_element_type=F32)
        s = jnp.where(causal, s, 0.0).astype(BF16)
        o_intra.append(jnp.dot(s, v_b[:, hh * GLA_DV:(hh + 1) * GLA_DV], preferred_element_type=F32))
    o_intra = jnp.concatenate(o_intra, axis=1)

    srow = lax.broadcasted_iota(jnp.int32, st_ref.shape, 0) // GLA_DV
    scol = lax.broadcasted_iota(jnp.int32, st_ref.shape, 1) // GLA_DK
    same_head = srow == scol
    q_b = q_e.astype(BF16)
    st = st_ref[...]
    o_inter = []
    for c in range(n_chunks):
        r0, r1 = c * GLA_CHUNK, (c + 1) * GLA_CHUNK
        o_inter.append(lax.dot_general(q_b[r0:r1], st.astype(BF16), (((1,), (1,)), ((), ())),
                                       preferred_element_type=F32))
        upd = lax.dot_general(v_b[r0:r1], k_d[r0:r1], (((0,), (0,)), ((), ())),
                              preferred_element_type=F32)
        decay = jnp.exp(b_last[r0:r0 + 1, :])
        st = jnp.where(same_head, st * decay + upd, 0.0)
    st_ref[...] = st
    o = o_intra + jnp.concatenate(o_inter, axis=0)

    outs = []
    for hh in range(GLA_HEADS):
        oh = o[:, hh * GLA_DV:(hh + 1) * GLA_DV]
        outs.append(oh * lax.rsqrt(jnp.mean(oh * oh, axis=-1, keepdims=True) + EPS) * gnw_ref[...])
    o = jnp.concatenate(outs, axis=1)
    o_ref[0] = (o * (g * (1.0 / (1.0 + jnp.exp(-g))))).astype(o_ref.dtype)


def _gla_call(x, nw, wg, wup, bgk, gnw):
    bsz, seq, _ = x.shape
    const = lambda shape: pl.BlockSpec(shape, lambda b, t: (0,) * len(shape))
    return pl.pallas_call(
        _gla_kernel,
        grid=(bsz, seq // GLA_BLOCK),
        in_specs=[
            pl.BlockSpec((1, GLA_BLOCK, D_MODEL), lambda b, t: (b, t, 0)),
            const(nw.shape), const(wg.shape), const(wup.shape), const(bgk.shape), const(gnw.shape),
        ],
        out_specs=pl.BlockSpec((1, GLA_BLOCK, GLA_WIDTH), lambda b, t: (b, t, 0)),
        out_shape=jax.ShapeDtypeStruct((bsz, seq, GLA_WIDTH), BF16),
        scratch_shapes=[pltpu.VMEM((GLA_HEADS * GLA_DV, GLA_HEADS * GLA_DK), F32)],
        compiler_params=pltpu.CompilerParams(
            dimension_semantics=("arbitrary", "arbitrary"), vmem_limit_bytes=VMEM_LIMIT_BYTES),
        name="gla_mixer",
    )(x, nw, wg, wup, bgk, gnw)


def _s5_kernel(x_ref, nw_ref, wu_ref, bt_ref, abr_ref, abi_ref, ct_ref, d_ref, wglu_ref, bglu_ref,
               o_ref, scan_ref, state_ref, u_ref, y_ref):
    bsz, L, _ = x_ref.shape
    P = S5_PITCH
    half = S5_SLABS // 2

    @pl.when(pl.program_id(0) == 0)
    def _():
        state_ref[...] = jnp.zeros_like(state_ref)
        scan_ref[...] = jnp.zeros_like(scan_ref)

    h = _rmsnorm(x_ref[...].reshape(bsz * L, D_MODEL), nw_ref[...])
    u = jnp.dot(h.astype(BF16), wu_ref[...], preferred_element_type=F32)
    u_ref[...] = u
    u_b = u.astype(BF16)

    n_tiles = bt_ref.shape[0]
    for j in range(n_tiles):
        ch = ((j % (n_tiles // 2)) * MXU_DIM // S5_STATE * S5_GROUP) // LANES * LANES
        bu = jnp.dot(u_b[:, ch:ch + LANES], bt_ref[j], preferred_element_type=F32)
        for bb in range(bsz):
            for e in range(MXU_DIM // LANES):
                scan_ref[2 * j + e, bb * P:bb * P + L, :] = bu[bb * L:(bb + 1) * L, e * LANES:(e + 1) * LANES]

    for s in range(half):
        ar = jnp.broadcast_to(abr_ref[s:s + 1, :], (bsz, LANES))
        ai = jnp.broadcast_to(abi_ref[s:s + 1, :], (bsz, LANES))

        def step(t, carry, s=s, ar=ar, ai=ai):
            xr, xi = carry
            rows = pl.ds(t, bsz, stride=P)
            nr = ar * xr - ai * xi + scan_ref[s, rows, :]
            ni = ar * xi + ai * xr + scan_ref[half + s, rows, :]
            scan_ref[s, rows, :] = nr
            scan_ref[half + s, rows, :] = ni
            return nr, ni

        xr, xi = lax.fori_loop(0, L, step, (state_ref[s], state_ref[half + s]), unroll=8)
        state_ref[s] = xr
        state_ref[half + s] = xi

    n_out = ct_ref.shape[0]
    per = half // n_out
    for i in range(n_out):
        slabs = list(range(i * per, (i + 1) * per)) + list(range(half + i * per, half + (i + 1) * per))
        xs = jnp.concatenate([scan_ref[s] for s in slabs], axis=1).astype(BF16)
        y_ref[:, i * MXU_DIM:(i + 1) * MXU_DIM] = jnp.dot(xs, ct_ref[i], preferred_element_type=F32)
    y = jnp.concatenate([y_ref[bb * P:bb * P + L, :] for bb in range(bsz)], axis=0)
    y = y + d_ref[...] * u_ref[...]

    z = 0.5 * y * (1.0 + jnp.tanh(np.float32(math.sqrt(2.0 / math.pi)) * (y + 0.044715 * (y * y * y))))
    gate = jnp.dot(z.astype(BF16), wglu_ref[...], preferred_element_type=F32) + bglu_ref[...]
    out = z * (1.0 / (1.0 + jnp.exp(-gate)))
    o_ref[...] = out.reshape(bsz, L, S5_WIDTH).astype(o_ref.dtype)


def _s5_call(x, nw, wu, bt, abr, abi, ct, d, wglu, bglu):
    bsz, seq, _ = x.shape
    L = S5_BLOCK
    const = lambda shape: pl.BlockSpec(shape, lambda t: (0,) * len(shape))
    return pl.pallas_call(
        _s5_kernel,
        grid=(seq // L,),
        in_specs=[pl.BlockSpec((bsz, L, D_MODEL), lambda t: (0, t, 0))]
        + [const(a.shape) for a in (nw, wu, bt, abr, abi, ct, d, wglu, bglu)],
        out_specs=pl.BlockSpec((bsz, L, S5_WIDTH), lambda t: (0, t, 0)),
        out_shape=jax.ShapeDtypeStruct((bsz, seq, S5_WIDTH), BF16),
        scratch_shapes=[
            pltpu.VMEM((S5_SLABS, bsz * S5_PITCH, LANES), F32),
            pltpu.VMEM((S5_SLABS, bsz, LANES), F32),
            pltpu.VMEM((bsz * L, S5_WIDTH), F32),
            pltpu.VMEM((bsz * S5_PITCH, S5_WIDTH), F32),
        ],
        compiler_params=pltpu.CompilerParams(
            dimension_semantics=("arbitrary",), vmem_limit_bytes=VMEM_LIMIT_BYTES),
        name="s5_mixer",
    )(x, nw, wu, bt, abr, abi, ct, d, wglu, bglu)


def _s5_discretize(a_re, a_im, log_dt, b_re, b_im, c_re, c_im):
    dt = jnp.exp(log_dt)[:, None]
    mag = jnp.exp(a_re * dt)
    abr = mag * jnp.cos(a_im * dt)
    abi = mag * jnp.sin(a_im * dt)
    den = a_re * a_re + a_im * a_im
    nr = abr - 1.0
    fr = (nr * a_re + abi * a_im) / den
    fi = (abi * a_re - nr * a_im) / den
    bbr = fr[..., None] * b_re - fi[..., None] * b_im
    bbi = fr[..., None] * b_im + fi[..., None] * b_re

    eye = jnp.eye(S5_GROUPS, dtype=F32)
    bd = lambda m: jnp.einsum('gnp,gh->gphn', m, eye).reshape(S5_WIDTH, S5_STATES)
    b_full = jnp.concatenate([bd(bbr), bd(bbi)], axis=1)
    cd = lambda m: jnp.einsum('gpn,gh->gnhp', m, eye).reshape(S5_STATES, S5_WIDTH)
    c_full = jnp.concatenate([cd(c_re), -cd(c_im)], axis=0)

    n_tiles = 2 * S5_STATES // MXU_DIM
    bt = []
    for j in range(n_tiles):
        ch = ((j % (n_tiles // 2)) * MXU_DIM // S5_STATE * S5_GROUP) // LANES * LANES
        bt.append(b_full[ch:ch + LANES, j * MXU_DIM:(j + 1) * MXU_DIM])
    bt = jnp.stack(bt).astype(BF16)

    n_out = S5_WIDTH // MXU_DIM
    per = S5_STATES // n_out
    ct = []
    for i in range(n_out):
        rows = jnp.concatenate([c_full[i * per:(i + 1) * per], c_full[S5_STATES + i * per:S5_STATES + (i + 1) * per]], axis=0)
        ct.append(rows[:, i * MXU_DIM:(i + 1) * MXU_DIM])
    ct = jnp.stack(ct).astype(BF16)
    return abr.reshape(S5_STATES // LANES, LANES), abi.reshape(S5_STATES // LANES, LANES), bt, ct


def _out_mlp_kernel(x_ref, og_ref, os_ref, wo_ref, nw_ref, wup_ref, wdn_ref, fw_ref, o_ref):
    x1 = x_ref[...] + jnp.dot(og_ref[...], wo_ref[0:GLA_WIDTH, :], preferred_element_type=F32) \
        + jnp.dot(os_ref[...], wo_ref[GLA_WIDTH:, :], preferred_element_type=F32)
    h = _rmsnorm(x1, nw_ref[...]).astype(BF16)
    acts = []
    for c in range(D_FF // MLP_FF_CHUNK):
        cs = slice(c * MLP_FF_CHUNK, (c + 1) * MLP_FF_CHUNK)
        a = jnp.maximum(jnp.dot(h, wup_ref[:, cs], preferred_element_type=F32), 0.0)
        acts.append((a * a).astype(BF16))
    x2 = x1 + jnp.dot(jnp.concatenate(acts, axis=1), wdn_ref[...], preferred_element_type=F32)
    o_ref[...] = _rmsnorm(x2, fw_ref[...])


def _out_mlp_call(x2, og, os_, wo, nw, wup, wdn, fw):
    rows = x2.shape[0]
    const = lambda shape: pl.BlockSpec(shape, lambda i: (0,) * len(shape))
    tile = lambda width: pl.BlockSpec((MLP_ROWS, width), lambda i: (i, 0))
    return pl.pallas_call(
        _out_mlp_kernel,
        grid=(rows // MLP_ROWS,),
        in_specs=[tile(D_MODEL), tile(GLA_WIDTH), tile(S5_WIDTH)]
        + [const(a.shape) for a in (wo, nw, wup, wdn, fw)],
        out_specs=tile(D_MODEL),
        out_shape=jax.ShapeDtypeStruct((rows, D_MODEL), F32),
        compiler_params=pltpu.CompilerParams(
            dimension_semantics=("arbitrary",), vmem_limit_bytes=VMEM_LIMIT_BYTES),
        name="out_mlp",
    )(x2, og, os_, wo, nw, wup, wdn, fw)


def kernel(x, norm_mix_w, w_in, w_gk_up, b_gk, gla_norm_w, s5_a_re, s5_a_im, s5_log_dt, s5_b_re, s5_b_im,
           s5_c_re, s5_c_im, s5_d, w_glu, b_glu, w_out, norm_mlp_w, w_mlp_up, w_mlp_down, norm_final_w):
    bsz, seq, _ = x.shape
    assert seq % GLA_BLOCK == 0 and seq % S5_BLOCK == 0 and (bsz * seq) % MLP_ROWS == 0
    assert norm_mix_w.shape[0] == 1, "single-layer problem"
    for l in range(1):
        w = w_in[l]
        kd2, r = 2 * GLA_KEY_DIM, GLA_GATE_RANK
        wg = jnp.concatenate([
            w[:, 0:kd2 + GLA_WIDTH], w[:, kd2 + GLA_WIDTH + r:kd2 + 2 * GLA_WIDTH + r],
            w[:, kd2 + GLA_WIDTH:kd2 + GLA_WIDTH + r], jnp.zeros((D_MODEL, LANES - r), F32)], axis=1).astype(BF16)
        wu = w[:, kd2 + 2 * GLA_WIDTH + r:].astype(BF16)
        wup = jnp.concatenate([w_gk_up[l], jnp.zeros((LANES - r, GLA_KEY_DIM), F32)], axis=0)
        nw = norm_mix_w[l][None, :]

        o_gla = _gla_call(x, nw, wg, wup, b_gk[l][None, :], gla_norm_w[l][None, :])

        abr, abi, bt, ct = _s5_discretize(s5_a_re[l], s5_a_im[l], s5_log_dt[l], s5_b_re[l], s5_b_im[l],
                                          s5_c_re[l], s5_c_im[l])
        o_s5 = _s5_call(x, nw, wu, bt, abr, abi, ct, s5_d[l].reshape(1, S5_WIDTH),
                        w_glu[l].astype(BF16), b_glu[l][None, :])

        rows = bsz * seq
        out = _out_mlp_call(x.reshape(rows, D_MODEL), o_gla.reshape(rows, GLA_WIDTH), o_s5.reshape(rows, S5_WIDTH),
                            w_out[l].astype(BF16), norm_mlp_w[l][None, :], w_mlp_up[l].astype(BF16),
                            w_mlp_down[l].astype(BF16), norm_final_w[None, :])
        x = out.reshape(bsz, seq, D_MODEL)
    return x
```

```python
import functools
import math

import jax
import jax.numpy as jnp
import numpy as np
from jax import lax
from jax.experimental import pallas as pl
from jax.experimental.pallas import tpu as pltpu

F32 = jnp.float32
BF16 = jnp.bfloat16

D_MODEL = 1024
GLA_WIDTH = 512
S5_WIDTH = 512
GLA_HEADS = 4
GLA_KEY_DIM = 256
GLA_DK = 64
GLA_DV = 128
GLA_GATE_RANK = 16
GLA_GATE_NORMALIZER = 16.0
GLA_CHUNK = 64
S5_GROUP = 16
S5_GROUPS = 32
S5_STATE = 64
D_FF = 4096
EPS = 1e-6

LANES = 128
MXU_DIM = 256
S5_STATES = S5_GROUPS * S5_STATE
VMEM_LIMIT_BYTES = 56 * 1024 * 1024

GLA_BLOCK = 256
S5_BLOCK = 64
S5_CHUNK = 4
S5_PITCH = S5_BLOCK + 8
MLP_ROWS = 512
MLP_FF_CHUNK = 1024


def _rmsnorm(x, w):
    return x * lax.rsqrt(jnp.mean(x * x, axis=-1, keepdims=True) + EPS) * w


def _gla_kernel(x_ref, nw_ref, wg_ref, wup_ref, bgk_ref, gnw_ref, o_ref, st_ref):
    L = GLA_BLOCK
    n_chunks = L // GLA_CHUNK

    @pl.when(pl.program_id(1) == 0)
    def _():
        st_ref[...] = jnp.zeros_like(st_ref)

    h = _rmsnorm(x_ref[0], nw_ref[...])
    p = jnp.dot(h.astype(BF16), wg_ref[...], preferred_element_type=F32)
    q = p[:, 0:256]
    k = p[:, 256:512]
    v = p[:, 512:1024]
    g = p[:, 1024:1536]
    gk_lr = p[:, 1536:1664]

    gk = jnp.dot(gk_lr, wup_ref[...], precision=lax.Precision.HIGHEST,
                 preferred_element_type=F32) + bgk_ref[...]
    log_a = (jnp.minimum(gk, 0.0) - jnp.log1p(jnp.exp(-jnp.abs(gk)))) * (1.0 / GLA_GATE_NORMALIZER)

    row = lax.broadcasted_iota(jnp.int32, (L, L), 0)
    col = lax.broadcasted_iota(jnp.int32, (L, L), 1)
    causal = ((row // GLA_CHUNK) == (col // GLA_CHUNK)) & (col <= row)
    tri = jnp.where(causal, 1.0, 0.0).astype(F32)
    b = jnp.dot(tri, log_a, precision=lax.Precision.HIGHEST, preferred_element_type=F32)
    b_last = jnp.concatenate(
        [jnp.broadcast_to(b[c * GLA_CHUNK + GLA_CHUNK - 1:(c + 1) * GLA_CHUNK, :], (GLA_CHUNK, GLA_KEY_DIM))
         for c in range(n_chunks)], axis=0)

    q_e = (q * (GLA_DK ** -0.5)) * jnp.exp(b)
    k_e = (k * jnp.exp(-b)).astype(BF16)
    k_d = (k * jnp.exp(b_last - b)).astype(BF16)
    v_b = v.astype(BF16)

    lane_head = lax.broadcasted_iota(jnp.int32, (L, GLA_KEY_DIM), 1) // GLA_DK
    o_intra = []
    for hh in range(GLA_HEADS):
        q_h = jnp.where(lane_head == hh, q_e, 0.0).astype(BF16)
        s = lax.dot_general(q_h, k_e, (((1,), (1,)), ((), ())), preferred_element_type=F32)
        s = jnp.where(causal, s, 0.0).astype(BF16)
        o_intra.append(jnp.dot(s, v_b[:, hh * GLA_DV:(hh + 1) * GLA_DV], preferred_element_type=F32))
    o_intra = jnp.concatenate(o_intra, axis=1)

    srow = lax.broadcasted_iota(jnp.int32, st_ref.shape, 0) // GLA_DV
    scol = lax.broadcasted_iota(jnp.int32, st_ref.shape, 1) // GLA_DK
    same_head = srow == scol
    q_b = q_e.astype(BF16)
    st = st_ref[...]
    o_inter = []
    for c in range(n_chunks):
        r0, r1 = c * GLA_CHUNK, (c + 1) * GLA_CHUNK
        o_inter.append(lax.dot_general(q_b[r0:r1], st.astype(BF16), (((1,), (1,)), ((), ())),
                                       preferred_element_type=F32))
        upd = lax.dot_general(v_b[r0:r1], k_d[r0:r1], (((0,), (0,)), ((), ())),
                              preferred_element_type=F32)
        decay = jnp.exp(b_last[r0:r0 + 1, :])
        st = jnp.where(same_head, st * decay + upd, 0.0)
    st_ref[...] = st
    o = o_intra + jnp.concatenate(o_inter, axis=0)

    outs = []
    for hh in range(GLA_HEADS):
        oh = o[:, hh * GLA_DV:(hh + 1) * GLA_DV]
        outs.append(oh * lax.rsqrt(jnp.mean(oh * oh, axis=-1, keepdims=True) + EPS) * gnw_ref[...])
    o = jnp.concatenate(outs, axis=1)
    o_ref[0] = (o * (g * (1.0 / (1.0 + jnp.exp(-g))))).astype(o_ref.dtype)


def _gla_call(x, nw, wg, wup, bgk, gnw):
    bsz, seq, _ = x.shape
    const = lambda shape: pl.BlockSpec(shape, lambda b, t: (0,) * len(shape))
    return pl.pallas_call(
        _gla_kernel,
        grid=(bsz, seq // GLA_BLOCK),
        in_specs=[
            pl.BlockSpec((1, GLA_BLOCK, D_MODEL), lambda b, t: (b, t, 0)),
            const(nw.shape), const(wg.shape), const(wup.shape), const(bgk.shape), const(gnw.shape),
        ],
        out_specs=pl.BlockSpec((1, GLA_BLOCK, GLA_WIDTH), lambda b, t: (b, t, 0)),
        out_shape=jax.ShapeDtypeStruct((bsz, seq, GLA_WIDTH), BF16),
        scratch_shapes=[pltpu.VMEM((GLA_HEADS * GLA_DV, GLA_HEADS * GLA_DK), F32)],
        compiler_params=pltpu.CompilerParams(
            dimension_semantics=("arbitrary", "arbitrary"), vmem_limit_bytes=VMEM_LIMIT_BYTES),
        name="gla_mixer",
    )(x, nw, wg, wup, bgk, gnw)


def _gelu_tanh(y):
    return 0.5 * y * (1.0 + jnp.tanh(np.float32(math.sqrt(2.0 / math.pi)) * (y + 0.044715 * (y * y * y))))


def _s5_kernel(x_ref, nw_ref, wu_ref, win_ref, wintra_ref, wout_ref, ajr_ref, aji_ref, dsk_ref, wglu_ref, bglu_ref,
               o_ref, u_ref, st_ref, carry_ref, y_ref, oscr_ref):
    bsz, L, _ = x_ref.shape
    J, P = S5_CHUNK, S5_PITCH
    n_c = L // J
    n_k = S5_WIDTH // LANES
    k_states = 2 * S5_STATES // n_k
    q_pairs = k_states // 2 // LANES

    @pl.when(pl.program_id(0) == 0)
    def _():
        carry_ref[...] = jnp.zeros_like(carry_ref)

    h = _rmsnorm(x_ref[...].reshape(bsz * L, D_MODEL), nw_ref[...])
    u = jnp.dot(h.astype(BF16), wu_ref[...], preferred_element_type=F32)
    for bb in range(bsz):
        for k in range(n_k):
            u_ref[k, bb * P:bb * P + L, :] = u[bb * L:(bb + 1) * L, k * LANES:(k + 1) * LANES]

    for k in range(n_k):
        ut = jnp.concatenate(
            [jnp.concatenate([u_ref[k, pl.ds(c * J + i, bsz, stride=P), :] for i in range(J)], axis=1)
             for c in range(n_c)], axis=0)
        utb = ut.astype(BF16)
        st_ref[:, k * k_states:(k + 1) * k_states] = jnp.dot(utb, win_ref[k], preferred_element_type=F32)
        y_ref[k] = jnp.dot(utb, wintra_ref[k], preferred_element_type=F32) + dsk_ref[k:k + 1, :] * ut

    for k in range(n_k):
        for q in range(q_pairs):
            lr = k * k_states + q * LANES
            li = lr + k_states // 2
            ar = jnp.broadcast_to(ajr_ref[k:k + 1, q * LANES:(q + 1) * LANES], (bsz, LANES))
            ai = jnp.broadcast_to(aji_ref[k:k + 1, q * LANES:(q + 1) * LANES], (bsz, LANES))
            xr = carry_ref[:, lr:lr + LANES]
            xi = carry_ref[:, li:li + LANES]
            for c in range(n_c):
                rows = slice(c * bsz, (c + 1) * bsz)
                sr = st_ref[rows, lr:lr + LANES]
                si = st_ref[rows, li:li + LANES]
                st_ref[rows, lr:lr + LANES] = xr
                st_ref[rows, li:li + LANES] = xi
                xr, xi = ar * xr - ai * xi + sr, ar * xi + ai * xr + si
            carry_ref[:, lr:lr + LANES] = xr
            carry_ref[:, li:li + LANES] = xi

    for k in range(n_k):
        xs = st_ref[:, k * k_states:(k + 1) * k_states].astype(BF16)
        y_ref[k] = _gelu_tanh(y_ref[k] + jnp.dot(xs, wout_ref[k], preferred_element_type=F32))

    for r in range(J):
        z = jnp.concatenate([y_ref[k, :, r * LANES:(r + 1) * LANES] for k in range(n_k)], axis=1)
        gate = jnp.dot(z.astype(BF16), wglu_ref[...], preferred_element_type=F32) + bglu_ref[...]
        o_r = z * (1.0 / (1.0 + jnp.exp(-gate)))
        for c in range(n_c):
            for k in range(n_k):
                oscr_ref[k, pl.ds(c * J + r, bsz, stride=P), :] = o_r[c * bsz:(c + 1) * bsz, k * LANES:(k + 1) * LANES]
    for bb in range(bsz):
        for k in range(n_k):
            o_ref[bb, :, k * LANES:(k + 1) * LANES] = oscr_ref[k, bb * P:bb * P + L, :].astype(o_ref.dtype)


def _s5_call(x, nw, wu, win, wintra, wout, ajr, aji, dsk, wglu, bglu):
    bsz, seq, _ = x.shape
    L, J = S5_BLOCK, S5_CHUNK
    rows_c = (L // J) * bsz
    n_k = S5_WIDTH // LANES
    const = lambda shape: pl.BlockSpec(shape, lambda t: (0,) * len(shape))
    return pl.pallas_call(
        _s5_kernel,
        grid=(seq // L,),
        in_specs=[pl.BlockSpec((bsz, L, D_MODEL), lambda t: (0, t, 0))]
        + [const(a.shape) for a in (nw, wu, win, wintra, wout, ajr, aji, dsk, wglu, bglu)],
        out_specs=pl.BlockSpec((bsz, L, S5_WIDTH), lambda t: (0, t, 0)),
        out_shape=jax.ShapeDtypeStruct((bsz, seq, S5_WIDTH), BF16),
        scratch_shapes=[
            pltpu.VMEM((n_k, bsz * S5_PITCH, LANES), F32),
            pltpu.VMEM((rows_c, 2 * S5_STATES), F32),
            pltpu.VMEM((bsz, 2 * S5_STATES), F32),
            pltpu.VMEM((n_k, rows_c, J * LANES), F32),
            pltpu.VMEM((n_k, bsz * S5_PITCH, LANES), F32),
        ],
        compiler_params=pltpu.CompilerParams(
            dimension_semantics=("arbitrary",), vmem_limit_bytes=VMEM_LIMIT_BYTES),
        name="s5_mixer",
    )(x, nw, wu, win, wintra, wout, ajr, aji, dsk, wglu, bglu)


def _s5_prepare(a_re, a_im, log_dt, b_re, b_im, c_re, c_im, d_skip):
    J = S5_CHUNK
    n_k = S5_WIDTH // LANES
    gpk = S5_GROUPS // n_k
    hi = lax.Precision.HIGHEST
    dt = jnp.exp(log_dt)[:, None]

    def apow(m):
        mag = jnp.exp(m * a_re * dt)
        return mag * jnp.cos(m * a_im * dt), mag * jnp.sin(m * a_im * dt)

    abr, abi = apow(1)
    den = a_re * a_re + a_im * a_im
    nr = abr - 1.0
    fr = (nr * a_re + abi * a_im) / den
    fi = (abi * a_re - nr * a_im) / den
    bbr = fr[..., None] * b_re - fi[..., None] * b_im
    bbi = fr[..., None] * b_im + fi[..., None] * b_re

    ab_re, ab_im = [], []
    for m in range(J):
        pr, pi = apow(m)
        ab_re.append(pr[..., None] * bbr - pi[..., None] * bbi)
        ab_im.append(pr[..., None] * bbi + pi[..., None] * bbr)

    eye = jnp.eye(gpk, dtype=F32)
    split = lambda t, axis: t.reshape(t.shape[:axis] + (n_k, gpk) + t.shape[axis + 1:])

    t_in = jnp.stack([jnp.stack([ab_re[J - 1 - i], ab_im[J - 1 - i]]) for i in range(J)])
    win = jnp.einsum('iakgnp,gh->kigpahn', split(t_in, 2), eye).reshape(n_k, J * LANES, 2 * gpk * S5_STATE)

    kmat = [jnp.einsum('gqn,gnp->gpq', c_re, ab_re[m], precision=hi)
            - jnp.einsum('gqn,gnp->gpq', c_im, ab_im[m], precision=hi) for m in range(J)]
    zero = jnp.zeros_like(kmat[0])
    t_intra = jnp.stack([jnp.stack([kmat[r - i] if i <= r else zero for r in range(J)]) for i in range(J)])
    wintra = jnp.einsum('irkgpq,gh->kigprhq', split(t_intra, 2), eye).reshape(n_k, J * LANES, J * LANES)

    t_out = []
    for r in range(J):
        pr, pi = apow(r + 1)
        t_out.append(jnp.stack([c_re * pr[:, None, :] - c_im * pi[:, None, :],
                                -(c_re * pi[:, None, :] + c_im * pr[:, None, :])]))
    wout = jnp.einsum('rakgqn,gh->kagnrhq', split(jnp.stack(t_out), 2), eye).reshape(
        n_k, 2 * gpk * S5_STATE, J * LANES)

    ajr, aji = apow(J)
    dsk = jnp.tile(d_skip.reshape(n_k, LANES), (1, J))
    return (win.astype(BF16), wintra.astype(BF16), wout.astype(BF16),
            ajr.reshape(n_k, gpk * S5_STATE), aji.reshape(n_k, gpk * S5_STATE), dsk)


def _out_mlp_kernel(x_ref, og_ref, os_ref, wo_ref, nw_ref, wup_ref, wdn_ref, fw_ref, o_ref):
    x1 = x_ref[...] + jnp.dot(og_ref[...], wo_ref[0:GLA_WIDTH, :], preferred_element_type=F32) \
        + jnp.dot(os_ref[...], wo_ref[GLA_WIDTH:, :], preferred_element_type=F32)
    h = _rmsnorm(x1, nw_ref[...]).astype(BF16)
    acts = []
    for c in range(D_FF // MLP_FF_CHUNK):
        cs = slice(c * MLP_FF_CHUNK, (c + 1) * MLP_FF_CHUNK)
        a = jnp.maximum(jnp.dot(h, wup_ref[:, cs], preferred_element_type=F32), 0.0)
        acts.append((a * a).astype(BF16))
    x2 = x1 + jnp.dot(jnp.concatenate(acts, axis=1), wdn_ref[...], preferred_element_type=F32)
    o_ref[...] = _rmsnorm(x2, fw_ref[...])


def _out_mlp_call(x2, og, os_, wo, nw, wup, wdn, fw):
    rows = x2.shape[0]
    const = lambda shape: pl.BlockSpec(shape, lambda i: (0,) * len(shape))
    tile = lambda width: pl.BlockSpec((MLP_ROWS, width), lambda i: (i, 0))
    return pl.pallas_call(
        _out_mlp_kernel,
        grid=(rows // MLP_ROWS,),
        in_specs=[tile(D_MODEL), tile(GLA_WIDTH), tile(S5_WIDTH)]
        + [const(a.shape) for a in (wo, nw, wup, wdn, fw)],
        out_specs=tile(D_MODEL),
        out_shape=jax.ShapeDtypeStruct((rows, D_MODEL), F32),
        compiler_params=pltpu.CompilerParams(
            dimension_semantics=("arbitrary",), vmem_limit_bytes=VMEM_LIMIT_BYTES),
        name="out_mlp",
    )(x2, og, os_, wo, nw, wup, wdn, fw)


def kernel(x, norm_mix_w, w_in, w_gk_up, b_gk, gla_norm_w, s5_a_re, s5_a_im, s5_log_dt, s5_b_re, s5_b_im,
           s5_c_re, s5_c_im, s5_d, w_glu, b_glu, w_out, norm_mlp_w, w_mlp_up, w_mlp_down, norm_final_w):
    bsz, seq, _ = x.shape
    assert seq % GLA_BLOCK == 0 and seq % S5_BLOCK == 0 and (bsz * seq) % MLP_ROWS == 0
    assert norm_mix_w.shape[0] == 1, "single-layer problem"
    for l in range(1):
        w = w_in[l]
        kd2, r = 2 * GLA_KEY_DIM, GLA_GATE_RANK
        wg = jnp.concatenate([
            w[:, 0:kd2 + GLA_WIDTH], w[:, kd2 + GLA_WIDTH + r:kd2 + 2 * GLA_WIDTH + r],
            w[:, kd2 + GLA_WIDTH:kd2 + GLA_WIDTH + r], jnp.zeros((D_MODEL, LANES - r), F32)], axis=1).astype(BF16)
        wu = w[:, kd2 + 2 * GLA_WIDTH + r:].astype(BF16)
        wup = jnp.concatenate([w_gk_up[l], jnp.zeros((LANES - r, GLA_KEY_DIM), F32)], axis=0)
        nw = norm_mix_w[l][None, :]

        o_gla = _gla_call(x, nw, wg, wup, b_gk[l][None, :], gla_norm_w[l][None, :])

        win, wintra, wout, ajr, aji, dsk = _s5_prepare(s5_a_re[l], s5_a_im[l], s5_log_dt[l], s5_b_re[l], s5_b_im[l],
                                                       s5_c_re[l], s5_c_im[l], s5_d[l])
        o_s5 = _s5_call(x, nw, wu, win, wintra, wout, ajr, aji, dsk, w_glu[l].astype(BF16), b_glu[l][None, :])

        rows = bsz * seq
        out = _out_mlp_call(x.reshape(rows, D_MODEL), o_gla.reshape(rows, GLA_WIDTH), o_s5.reshape(rows, S5_WIDTH),
                            w_out[l].astype(BF16), norm_mlp_w[l][None, :], w_mlp_up[l].astype(BF16),
                            w_mlp_down[l].astype(BF16), norm_final_w[None, :])
        x = out.reshape(bsz, seq, D_MODEL)
    return x
```

```python
import functools
import math

import jax
import jax.numpy as jnp
import numpy as np
from jax import lax
from jax.experimental import pallas as pl
from jax.experimental.pallas import tpu as pltpu

F32 = jnp.float32
BF16 = jnp.bfloat16

D_MODEL = 1024
GLA_WIDTH = 512
S5_WIDTH = 512
GLA_HEADS = 4
GLA_KEY_DIM = 256
GLA_DK = 64
GLA_DV = 128
GLA_GATE_RANK = 16
GLA_GATE_NORMALIZER = 16.0
GLA_CHUNK = 64
S5_GROUP = 16
S5_GROUPS = 32
S5_STATE = 64
D_FF = 4096
EPS = 1e-6

LANES = 128
MXU_DIM = 256
S5_STATES = S5_GROUPS * S5_STATE
VMEM_LIMIT_BYTES = 56 * 1024 * 1024

GLA_BLOCK = 256
GLA_SEQS = 2
S5_BLOCK = 64
S5_CHUNK = 4
S5_PITCH = S5_BLOCK + 8
MLP_ROWS = 512
MLP_FF_CHUNK = 1024


def _rmsnorm(x, w):
    return x * lax.rsqrt(jnp.mean(x * x, axis=-1, keepdims=True) + EPS) * w


def _bf16_parts(x, n):
    parts = []
    for _ in range(n):
        part = x.astype(BF16)
        parts.append(part)
        x = x - part.astype(F32)
    return parts


def _gla_kernel(x_ref, nw_ref, wg_ref, wup_ref, bgk_ref, gnw_ref, o_ref, st_ref):
    nb, L, _ = x_ref.shape
    n_chunks = L // GLA_CHUNK
    seqs = range(nb)

    @pl.when(pl.program_id(1) == 0)
    def _():
        st_ref[...] = jnp.zeros_like(st_ref)

    h = _rmsnorm(x_ref[...].reshape(nb * L, D_MODEL), nw_ref[...])
    p = jnp.dot(h.astype(BF16), wg_ref[...], preferred_element_type=F32)
    cols = lambda i, c0, c1: p[i * L:(i + 1) * L, c0:c1]

    w_hi, w_lo = _bf16_parts(wup_ref[...], 2)
    w_cat = jnp.concatenate([w_hi, w_hi], axis=0)
    row = lax.broadcasted_iota(jnp.int32, (L, L), 0)
    col = lax.broadcasted_iota(jnp.int32, (L, L), 1)
    causal = ((row // GLA_CHUNK) == (col // GLA_CHUNK)) & (col <= row)
    tri = jnp.where(causal, 1.0, 0.0).astype(BF16)
    srow = lax.broadcasted_iota(jnp.int32, st_ref.shape[1:], 0) // GLA_DV
    scol = lax.broadcasted_iota(jnp.int32, st_ref.shape[1:], 1) // GLA_DK
    same_head = srow == scol
    lane_head = lax.broadcasted_iota(jnp.int32, (L, GLA_KEY_DIM), 1) // GLA_DK

    log_a = []
    for i in seqs:
        gl_hi, gl_lo = _bf16_parts(cols(i, 1536, 1664), 2)
        gk = (jnp.dot(jnp.concatenate([gl_hi, gl_lo], axis=1), w_cat, preferred_element_type=F32)
              + jnp.dot(gl_hi, w_lo, preferred_element_type=F32) + bgk_ref[...])
        log_a.append((jnp.minimum(gk, 0.0) - jnp.log1p(jnp.exp(-jnp.abs(gk)))) * (1.0 / GLA_GATE_NORMALIZER))

    b = [sum(jnp.dot(tri, part, preferred_element_type=F32) for part in _bf16_parts(log_a[i], 3)) for i in seqs]
    b_last = [jnp.concatenate(
        [jnp.broadcast_to(b[i][c * GLA_CHUNK + GLA_CHUNK - 1:(c + 1) * GLA_CHUNK, :], (GLA_CHUNK, GLA_KEY_DIM))
         for c in range(n_chunks)], axis=0) for i in seqs]

    q_e = [(cols(i, 0, 256) * (GLA_DK ** -0.5)) * jnp.exp(b[i]) for i in seqs]
    k_e = [(cols(i, 256, 512) * jnp.exp(-b[i])).astype(BF16) for i in seqs]
    k_d = [(cols(i, 256, 512) * jnp.exp(b_last[i] - b[i])).astype(BF16) for i in seqs]
    v_b = [cols(i, 512, 1024).astype(BF16) for i in seqs]

    o_intra = [[] for _ in seqs]
    for hh in range(GLA_HEADS):
        for i in seqs:
            q_h = jnp.where(lane_head == hh, q_e[i], 0.0).astype(BF16)
            s = lax.dot_general(q_h, k_e[i], (((1,), (1,)), ((), ())), preferred_element_type=F32)
            s = jnp.where(causal, s, 0.0).astype(BF16)
            o_intra[i].append(jnp.dot(s, v_b[i][:, hh * GLA_DV:(hh + 1) * GLA_DV], preferred_element_type=F32))

    q_b = [q_e[i].astype(BF16) for i in seqs]
    st = [st_ref[i] for i in seqs]
    o_inter = [[] for _ in seqs]
    for c in range(n_chunks):
        r0, r1 = c * GLA_CHUNK, (c + 1) * GLA_CHUNK
        for i in seqs:
            o_inter[i].append(lax.dot_general(q_b[i][r0:r1], st[i].astype(BF16), (((1,), (1,)), ((), ())),
                                              preferred_element_type=F32))
            upd = lax.dot_general(v_b[i][r0:r1], k_d[i][r0:r1], (((0,), (0,)), ((), ())),
                                  preferred_element_type=F32)
            decay = jnp.exp(b_last[i][r0:r0 + 1, :])
            st[i] = jnp.where(same_head, st[i] * decay + upd, 0.0)

    for i in seqs:
        st_ref[i] = st[i]
        o = jnp.concatenate(o_intra[i], axis=1) + jnp.concatenate(o_inter[i], axis=0)
        outs = []
        for hh in range(GLA_HEADS):
            oh = o[:, hh * GLA_DV:(hh + 1) * GLA_DV]
            outs.append(oh * lax.rsqrt(jnp.mean(oh * oh, axis=-1, keepdims=True) + EPS) * gnw_ref[...])
        g = cols(i, 1024, 1536)
        o_ref[i] = (jnp.concatenate(outs, axis=1) * (g * (1.0 / (1.0 + jnp.exp(-g))))).astype(o_ref.dtype)


def _gla_call(x, nw, wg, wup, bgk, gnw):
    bsz, seq, _ = x.shape
    nb = GLA_SEQS
    const = lambda shape: pl.BlockSpec(shape, lambda b, t: (0,) * len(shape))
    return pl.pallas_call(
        _gla_kernel,
        grid=(bsz // nb, seq // GLA_BLOCK),
        in_specs=[
            pl.BlockSpec((nb, GLA_BLOCK, D_MODEL), lambda b, t: (b, t, 0)),
            const(nw.shape), const(wg.shape), const(wup.shape), const(bgk.shape), const(gnw.shape),
        ],
        out_specs=pl.BlockSpec((nb, GLA_BLOCK, GLA_WIDTH), lambda b, t: (b, t, 0)),
        out_shape=jax.ShapeDtypeStruct((bsz, seq, GLA_WIDTH), BF16),
        scratch_shapes=[pltpu.VMEM((nb, GLA_HEADS * GLA_DV, GLA_HEADS * GLA_DK), F32)],
        compiler_params=pltpu.CompilerParams(
            dimension_semantics=("arbitrary", "arbitrary"), vmem_limit_bytes=VMEM_LIMIT_BYTES),
        name="gla_mixer",
    )(x, nw, wg, wup, bgk, gnw)


def _gelu_tanh(y):
    return 0.5 * y * (1.0 + jnp.tanh(np.float32(math.sqrt(2.0 / math.pi)) * (y + 0.044715 * (y * y * y))))


def _s5_kernel(x_ref, nw_ref, wu_ref, win_ref, wintra_ref, wout_ref, ajr_ref, aji_ref, dsk_ref, wglu_ref, bglu_ref,
               o_ref, u_ref, st_ref, carry_ref, y_ref, oscr_ref):
    bsz, L, _ = x_ref.shape
    J, P = S5_CHUNK, S5_PITCH
    n_c = L // J
    n_k = S5_WIDTH // LANES
    k_states = 2 * S5_STATES // n_k
    q_pairs = k_states // 2 // LANES

    @pl.when(pl.program_id(0) == 0)
    def _():
        carry_ref[...] = jnp.zeros_like(carry_ref)

    h = _rmsnorm(x_ref[...].reshape(bsz * L, D_MODEL), nw_ref[...])
    u = jnp.dot(h.astype(BF16), wu_ref[...], preferred_element_type=F32)
    for bb in range(bsz):
        for k in range(n_k):
            u_ref[k, bb * P:bb * P + L, :] = u[bb * L:(bb + 1) * L, k * LANES:(k + 1) * LANES]

    for k in range(n_k):
        ut = jnp.concatenate(
            [jnp.concatenate([u_ref[k, pl.ds(c * J + i, bsz, stride=P), :] for i in range(J)], axis=1)
             for c in range(n_c)], axis=0)
        utb = ut.astype(BF16)
        st_ref[:, k * k_states:(k + 1) * k_states] = jnp.dot(utb, win_ref[k], preferred_element_type=F32)
        y_ref[k] = jnp.dot(utb, wintra_ref[k], preferred_element_type=F32) + dsk_ref[k:k + 1, :] * ut

    for k in range(n_k):
        for q in range(q_pairs):
            lr = k * k_states + q * LANES
            li = lr + k_states // 2
            ar = jnp.broadcast_to(ajr_ref[k:k + 1, q * LANES:(q + 1) * LANES], (bsz, LANES))
            ai = jnp.broadcast_to(aji_ref[k:k + 1, q * LANES:(q + 1) * LANES], (bsz, LANES))
            xr = carry_ref[:, lr:lr + LANES]
            xi = carry_ref[:, li:li + LANES]
            for c in range(n_c):
                rows = slice(c * bsz, (c + 1) * bsz)
                sr = st_ref[rows, lr:lr + LANES]
                si = st_ref[rows, li:li + LANES]
                st_ref[rows, lr:lr + LANES] = xr
                st_ref[rows, li:li + LANES] = xi
                xr, xi = ar * xr - ai * xi + sr, ar * xi + ai * xr + si
            carry_ref[:, lr:lr + LANES] = xr
            carry_ref[:, li:li + LANES] = xi

    for k in range(n_k):
        xs = st_ref[:, k * k_states:(k + 1) * k_states].astype(BF16)
        y_ref[k] = _gelu_tanh(y_ref[k] + jnp.dot(xs, wout_ref[k], preferred_element_type=F32))

    for r in range(J):
        z = jnp.concatenate([y_ref[k, :, r * LANES:(r + 1) * LANES] for k in range(n_k)], axis=1)
        gate = jnp.dot(z.astype(BF16), wglu_ref[...], preferred_element_type=F32) + bglu_ref[...]
        o_r = z * (1.0 / (1.0 + jnp.exp(-gate)))
        for c in range(n_c):
            for k in range(n_k):
                oscr_ref[k, pl.ds(c * J + r, bsz, stride=P), :] = o_r[c * bsz:(c + 1) * bsz, k * LANES:(k + 1) * LANES]
    for bb in range(bsz):
        for k in range(n_k):
            o_ref[bb, :, k * LANES:(k + 1) * LANES] = oscr_ref[k, bb * P:bb * P + L, :].astype(o_ref.dtype)


def _s5_call(x, nw, wu, win, wintra, wout, ajr, aji, dsk, wglu, bglu):
    bsz, seq, _ = x.shape
    L, J = S5_BLOCK, S5_CHUNK
    rows_c = (L // J) * bsz
    n_k = S5_WIDTH // LANES
    const = lambda shape: pl.BlockSpec(shape, lambda t: (0,) * len(shape))
    return pl.pallas_call(
        _s5_kernel,
        grid=(seq // L,),
        in_specs=[pl.BlockSpec((bsz, L, D_MODEL), lambda t: (0, t, 0))]
        + [const(a.shape) for a in (nw, wu, win, wintra, wout, ajr, aji, dsk, wglu, bglu)],
        out_specs=pl.BlockSpec((bsz, L, S5_WIDTH), lambda t: (0, t, 0)),
        out_shape=jax.ShapeDtypeStruct((bsz, seq, S5_WIDTH), BF16),
        scratch_shapes=[
            pltpu.VMEM((n_k, bsz * S5_PITCH, LANES), F32),
            pltpu.VMEM((rows_c, 2 * S5_STATES), F32),
            pltpu.VMEM((bsz, 2 * S5_STATES), F32),
            pltpu.VMEM((n_k, rows_c, J * LANES), F32),
            pltpu.VMEM((n_k, bsz * S5_PITCH, LANES), F32),
        ],
        compiler_params=pltpu.CompilerParams(
            dimension_semantics=("arbitrary",), vmem_limit_bytes=VMEM_LIMIT_BYTES),
        name="s5_mixer",
    )(x, nw, wu, win, wintra, wout, ajr, aji, dsk, wglu, bglu)


def _s5_prepare(a_re, a_im, log_dt, b_re, b_im, c_re, c_im, d_skip):
    J = S5_CHUNK
    n_k = S5_WIDTH // LANES
    gpk = S5_GROUPS // n_k
    hi = lax.Precision.HIGHEST
    dt = jnp.exp(log_dt)[:, None]

    def apow(m):
        mag = jnp.exp(m * a_re * dt)
        return mag * jnp.cos(m * a_im * dt), mag * jnp.sin(m * a_im * dt)

    abr, abi = apow(1)
    den = a_re * a_re + a_im * a_im
    nr = abr - 1.0
    fr = (nr * a_re + abi * a_im) / den
    fi = (abi * a_re - nr * a_im) / den
    bbr = fr[..., None] * b_re - fi[..., None] * b_im
    bbi = fr[..., None] * b_im + fi[..., None] * b_re

    ab_re, ab_im = [], []
    for m in range(J):
        pr, pi = apow(m)
        ab_re.append(pr[..., None] * bbr - pi[..., None] * bbi)
        ab_im.append(pr[..., None] * bbi + pi[..., None] * bbr)

    eye = jnp.eye(gpk, dtype=F32)
    split = lambda t, axis: t.reshape(t.shape[:axis] + (n_k, gpk) + t.shape[axis + 1:])

    t_in = jnp.stack([jnp.stack([ab_re[J - 1 - i], ab_im[J - 1 - i]]) for i in range(J)])
    win = jnp.einsum('iakgnp,gh->kigpahn', split(t_in, 2), eye).reshape(n_k, J * LANES, 2 * gpk * S5_STATE)

    kmat = [jnp.einsum('gqn,gnp->gpq', c_re, ab_re[m], precision=hi)
            - jnp.einsum('gqn,gnp->gpq', c_im, ab_im[m], precision=hi) for m in range(J)]
    zero = jnp.zeros_like(kmat[0])
    t_intra = jnp.stack([jnp.stack([kmat[r - i] if i <= r else zero for r in range(J)]) for i in range(J)])
    wintra = jnp.einsum('irkgpq,gh->kigprhq', split(t_intra, 2), eye).reshape(n_k, J * LANES, J * LANES)

    t_out = []
    for r in range(J):
        pr, pi = apow(r + 1)
        t_out.append(jnp.stack([c_re * pr[:, None, :] - c_im * pi[:, None, :],
                                -(c_re * pi[:, None, :] + c_im * pr[:, None, :])]))
    wout = jnp.einsum('rakgqn,gh->kagnrhq', split(jnp.stack(t_out), 2), eye).reshape(
        n_k, 2 * gpk * S5_STATE, J * LANES)

    ajr, aji = apow(J)
    dsk = jnp.tile(d_skip.reshape(n_k, LANES), (1, J))
    return (win.astype(BF16), wintra.astype(BF16), wout.astype(BF16),
            ajr.reshape(n_k, gpk * S5_STATE), aji.reshape(n_k, gpk * S5_STATE), dsk)


def _out_mlp_kernel(x_ref, og_ref, os_ref, wo_ref, nw_ref, wup_ref, wdn_ref, fw_ref, o_ref):
    x1 = x_ref[...] + jnp.dot(og_ref[...], wo_ref[0:GLA_WIDTH, :], preferred_element_type=F32) \
        + jnp.dot(os_ref[...], wo_ref[GLA_WIDTH:, :], preferred_element_type=F32)
    h = _rmsnorm(x1, nw_ref[...]).astype(BF16)
    acts = []
    for c in range(D_FF // MLP_FF_CHUNK):
        cs = slice(c * MLP_FF_CHUNK, (c + 1) * MLP_FF_CHUNK)
        a = jnp.maximum(jnp.dot(h, wup_ref[:, cs], preferred_element_type=F32), 0.0)
        acts.append((a * a).astype(BF16))
    x2 = x1 + jnp.dot(jnp.concatenate(acts, axis=1), wdn_ref[...], preferred_element_type=F32)
    o_ref[...] = _rmsnorm(x2, fw_ref[...])


def _out_mlp_call(x2, og, os_, wo, nw, wup, wdn, fw):
    rows = x2.shape[0]
    const = lambda shape: pl.BlockSpec(shape, lambda i: (0,) * len(shape))
    tile = lambda width: pl.BlockSpec((MLP_ROWS, width), lambda i: (i, 0))
    return pl.pallas_call(
        _out_mlp_kernel,
        grid=(rows // MLP_ROWS,),
        in_specs=[tile(D_MODEL), tile(GLA_WIDTH), tile(S5_WIDTH)]
        + [const(a.shape) for a in (wo, nw, wup, wdn, fw)],
        out_specs=tile(D_MODEL),
        out_shape=jax.ShapeDtypeStruct((rows, D_MODEL), F32),
        compiler_params=pltpu.CompilerParams(
            dimension_semantics=("arbitrary",), vmem_limit_bytes=VMEM_LIMIT_BYTES),
        name="out_mlp",
    )(x2, og, os_, wo, nw, wup, wdn, fw)


def kernel(x, norm_mix_w, w_in, w_gk_up, b_gk, gla_norm_w, s5_a_re, s5_a_im, s5_log_dt, s5_b_re, s5_b_im,
           s5_c_re, s5_c_im, s5_d, w_glu, b_glu, w_out, norm_mlp_w, w_mlp_up, w_mlp_down, norm_final_w):
    bsz, seq, _ = x.shape
    assert bsz % GLA_SEQS == 0 and seq % GLA_BLOCK == 0 and seq % S5_BLOCK == 0 and (bsz * seq) % MLP_ROWS == 0
    assert norm_mix_w.shape[0] == 1, "single-layer problem"
    for l in range(1):
        w = w_in[l]
        kd2, r = 2 * GLA_KEY_DIM, GLA_GATE_RANK
        wg = jnp.concatenate([
            w[:, 0:kd2 + GLA_WIDTH], w[:, kd2 + GLA_WIDTH + r:kd2 + 2 * GLA_WIDTH + r],
            w[:, kd2 + GLA_WIDTH:kd2 + GLA_WIDTH + r], jnp.zeros((D_MODEL, LANES - r), F32)], axis=1).astype(BF16)
        wu = w[:, kd2 + 2 * GLA_WIDTH + r:].astype(BF16)
        wup = jnp.concatenate([w_gk_up[l], jnp.zeros((LANES - r, GLA_KEY_DIM), F32)], axis=0)
        nw = norm_mix_w[l][None, :]

        o_gla = _gla_call(x, nw, wg, wup, b_gk[l][None, :], gla_norm_w[l][None, :])

        win, wintra, wout, ajr, aji, dsk = _s5_prepare(s5_a_re[l], s5_a_im[l], s5_log_dt[l], s5_b_re[l], s5_b_im[l],
                                                       s5_c_re[l], s5_c_im[l], s5_d[l])
        o_s5 = _s5_call(x, nw, wu, win, wintra, wout, ajr, aji, dsk, w_glu[l].astype(BF16), b_glu[l][None, :])

        rows = bsz * seq
        out = _out_mlp_call(x.reshape(rows, D_MODEL), o_gla.reshape(rows, GLA_WIDTH), o_s5.reshape(rows, S5_WIDTH),
                            w_out[l].astype(BF16), norm_mlp_w[l][None, :], w_mlp_up[l].astype(BF16),
                            w_mlp_down[l].astype(BF16), norm_final_w[None, :])
        x = out.reshape(bsz, seq, D_MODEL)
    return x
```

```python
import functools
import math

import jax
import jax.numpy as jnp
import numpy as np
from jax import lax
from jax.experimental import pallas as pl
from jax.experimental.pallas import tpu as pltpu

F32 = jnp.float32
BF16 = jnp.bfloat16

D_MODEL = 1024
GLA_WIDTH = 512
S5_WIDTH = 512
GLA_HEADS = 4
GLA_KEY_DIM = 256
GLA_DK = 64
GLA_DV = 128
GLA_GATE_RANK = 16
GLA_GATE_NORMALIZER = 16.0
GLA_CHUNK = 64
S5_GROUP = 16
S5_GROUPS = 32
S5_STATE = 64
D_FF = 4096
EPS = 1e-6

LANES = 128
MXU_DIM = 256
S5_STATES = S5_GROUPS * S5_STATE
VMEM_LIMIT_BYTES = 56 * 1024 * 1024

GLA_BLOCK = 256
GLA_SEQS = 2
S5_BLOCK = 64
S5_CHUNK = 4
S5_PITCH = S5_BLOCK + 8
MLP_ROWS = 512
MLP_FF_CHUNK = 1024


def _rmsnorm(x, w):
    return x * lax.rsqrt(jnp.mean(x * x, axis=-1, keepdims=True) + EPS) * w


def _bf16_parts(x, n):
    parts = []
    for _ in range(n):
        part = x.astype(BF16)
        parts.append(part)
        x = x - part.astype(F32)
    return parts


def _gla_substep(x_ref, p_new_ref, p_ref, st_ref, o_ref, first, nw_ref, wg_ref, w_cat, w_lo, bgk_ref, gnw_ref,
                 causal, tri, same_head, lane_head):
    nb, L, _ = x_ref.shape
    n_chunks = L // GLA_CHUNK
    seqs = range(nb)
    cols = lambda i, c0, c1: p_ref[i * L:(i + 1) * L, c0:c1]

    hb = _rmsnorm(x_ref[...].reshape(nb * L, D_MODEL), nw_ref[...]).astype(BF16)
    n_cols = wg_ref.shape[1]
    tiles = [(c0, min(c0 + MXU_DIM, n_cols)) for c0 in range(0, n_cols, MXU_DIM)]

    def project_tile():
        if tiles:
            c0, c1 = tiles.pop(0)
            p_new_ref[:, c0:c1] = jnp.dot(hb, wg_ref[:, c0:c1], preferred_element_type=F32)

    log_a = []
    for i in seqs:
        gl_hi, gl_lo = _bf16_parts(cols(i, 1536, 1664), 2)
        gk = (jnp.dot(jnp.concatenate([gl_hi, gl_lo], axis=1), w_cat, preferred_element_type=F32)
              + jnp.dot(gl_hi, w_lo, preferred_element_type=F32) + bgk_ref[...])
        log_a.append((jnp.minimum(gk, 0.0) - jnp.log1p(jnp.exp(-jnp.abs(gk)))) * (1.0 / GLA_GATE_NORMALIZER))
    project_tile()

    b = [sum(jnp.dot(tri, part, preferred_element_type=F32) for part in _bf16_parts(log_a[i], 3)) for i in seqs]
    b_last = [jnp.concatenate(
        [jnp.broadcast_to(b[i][c * GLA_CHUNK + GLA_CHUNK - 1:(c + 1) * GLA_CHUNK, :], (GLA_CHUNK, GLA_KEY_DIM))
         for c in range(n_chunks)], axis=0) for i in seqs]
    project_tile()

    q_e = [(cols(i, 0, 256) * (GLA_DK ** -0.5)) * jnp.exp(b[i]) for i in seqs]
    k_e = [(cols(i, 256, 512) * jnp.exp(-b[i])).astype(BF16) for i in seqs]
    k_d = [(cols(i, 256, 512) * jnp.exp(b_last[i] - b[i])).astype(BF16) for i in seqs]
    v_b = [cols(i, 512, 1024).astype(BF16) for i in seqs]
    project_tile()

    o_intra = [[] for _ in seqs]
    for hh in range(GLA_HEADS):
        for i in seqs:
            q_h = jnp.where(lane_head == hh, q_e[i], 0.0).astype(BF16)
            s = lax.dot_general(q_h, k_e[i], (((1,), (1,)), ((), ())), preferred_element_type=F32)
            s = jnp.where(causal, s, 0.0).astype(BF16)
            o_intra[i].append(jnp.dot(s, v_b[i][:, hh * GLA_DV:(hh + 1) * GLA_DV], preferred_element_type=F32))
        project_tile()

    q_b = [q_e[i].astype(BF16) for i in seqs]
    st = [jnp.where(first, 0.0, st_ref[i]) for i in seqs]
    o_inter = [[] for _ in seqs]
    for c in range(n_chunks):
        r0, r1 = c * GLA_CHUNK, (c + 1) * GLA_CHUNK
        for i in seqs:
            o_inter[i].append(lax.dot_general(q_b[i][r0:r1], st[i].astype(BF16), (((1,), (1,)), ((), ())),
                                              preferred_element_type=F32))
            upd = lax.dot_general(v_b[i][r0:r1], k_d[i][r0:r1], (((0,), (0,)), ((), ())),
                                  preferred_element_type=F32)
            decay = jnp.exp(b_last[i][r0:r0 + 1, :])
            st[i] = jnp.where(same_head, st[i] * decay + upd, 0.0)
        project_tile()

    for i in seqs:
        st_ref[i] = st[i]
        o = jnp.concatenate(o_intra[i], axis=1) + jnp.concatenate(o_inter[i], axis=0)
        outs = []
        for hh in range(GLA_HEADS):
            oh = o[:, hh * GLA_DV:(hh + 1) * GLA_DV]
            outs.append(oh * lax.rsqrt(jnp.mean(oh * oh, axis=-1, keepdims=True) + EPS) * gnw_ref[...])
        g = cols(i, 1024, 1536)
        o_ref[i] = (jnp.concatenate(outs, axis=1) * (g * (1.0 / (1.0 + jnp.exp(-g))))).astype(o_ref.dtype)
    while tiles:
        project_tile()


def _gla_kernel(blocks_per_seq, xa_ref, xb_ref, nw_ref, wg_ref, wup_ref, bgk_ref, gnw_ref, o_ref,
                st_ref, pa_ref, pb_ref):
    s = pl.program_id(0)
    L = xa_ref.shape[1]

    @pl.when(s == 0)
    def _():
        st_ref[...] = jnp.zeros_like(st_ref)
        pb_ref[...] = jnp.zeros_like(pb_ref)

    w_hi, w_lo = _bf16_parts(wup_ref[...], 2)
    w_cat = jnp.concatenate([w_hi, w_hi], axis=0)
    row = lax.broadcasted_iota(jnp.int32, (L, L), 0)
    col = lax.broadcasted_iota(jnp.int32, (L, L), 1)
    causal = ((row // GLA_CHUNK) == (col // GLA_CHUNK)) & (col <= row)
    tri = jnp.where(causal, 1.0, 0.0).astype(BF16)
    srow = lax.broadcasted_iota(jnp.int32, st_ref.shape[1:], 0) // GLA_DV
    scol = lax.broadcasted_iota(jnp.int32, st_ref.shape[1:], 1) // GLA_DK
    same_head = srow == scol
    lane_head = lax.broadcasted_iota(jnp.int32, (L, GLA_KEY_DIM), 1) // GLA_DK
    shared = (nw_ref, wg_ref, w_cat, w_lo, bgk_ref, gnw_ref, causal, tri, same_head, lane_head)

    _gla_substep(xa_ref, pa_ref, pb_ref, st_ref, o_ref.at[0], (2 * s - 1) % blocks_per_seq == 0, *shared)
    _gla_substep(xb_ref, pb_ref, pa_ref, st_ref, o_ref.at[1], (2 * s) % blocks_per_seq == 0, *shared)


def _gla_call(x, nw, wg, wup, bgk, gnw):
    bsz, seq, _ = x.shape
    nb, L = GLA_SEQS, GLA_BLOCK
    bps = seq // L
    n_blocks = (bsz // nb) * bps
    assert n_blocks % 2 == 0 and bps % 2 == 0
    const = lambda shape: pl.BlockSpec(shape, lambda s: (0,) * len(shape))

    def x_spec(e):
        def index(s):
            n = jnp.minimum(2 * s + e, n_blocks - 1)
            return (n // bps, n % bps, 0)
        return pl.BlockSpec((nb, L, D_MODEL), index)

    return pl.pallas_call(
        functools.partial(_gla_kernel, bps),
        grid=(n_blocks // 2 + 1,),
        in_specs=[x_spec(0), x_spec(1),
                  const(nw.shape), const(wg.shape), const(wup.shape), const(bgk.shape), const(gnw.shape)],
        out_specs=pl.BlockSpec((2, nb, L, GLA_WIDTH), lambda s: (s, 0, 0, 0)),
        out_shape=jax.ShapeDtypeStruct((n_blocks + 2, nb, L, GLA_WIDTH), BF16),
        scratch_shapes=[pltpu.VMEM((nb, GLA_HEADS * GLA_DV, GLA_HEADS * GLA_DK), F32),
                        pltpu.VMEM((nb * L, wg.shape[1]), F32),
                        pltpu.VMEM((nb * L, wg.shape[1]), F32)],
        compiler_params=pltpu.CompilerParams(
            dimension_semantics=("arbitrary",), vmem_limit_bytes=VMEM_LIMIT_BYTES),
        name="gla_mixer",
    )(x, x, nw, wg, wup, bgk, gnw)


def _gelu_tanh(y):
    return 0.5 * y * (1.0 + jnp.tanh(np.float32(math.sqrt(2.0 / math.pi)) * (y + 0.044715 * (y * y * y))))


def _s5_kernel(x_ref, nw_ref, wu_ref, win_ref, wintra_ref, wout_ref, ajr_ref, aji_ref, dsk_ref, wglu_ref, bglu_ref,
               o_ref, u_ref, st_ref, carry_ref, y_ref, oscr_ref):
    bsz, L, _ = x_ref.shape
    J, P = S5_CHUNK, S5_PITCH
    n_c = L // J
    n_k = S5_WIDTH // LANES
    k_states = 2 * S5_STATES // n_k
    q_pairs = k_states // 2 // LANES

    @pl.when(pl.program_id(0) == 0)
    def _():
        carry_ref[...] = jnp.zeros_like(carry_ref)

    h = _rmsnorm(x_ref[...].reshape(bsz * L, D_MODEL), nw_ref[...])
    u = jnp.dot(h.astype(BF16), wu_ref[...], preferred_element_type=F32)
    for bb in range(bsz):
        for k in range(n_k):
            u_ref[k, bb * P:bb * P + L, :] = u[bb * L:(bb + 1) * L, k * LANES:(k + 1) * LANES]

    for k in range(n_k):
        ut = jnp.concatenate(
            [jnp.concatenate([u_ref[k, pl.ds(c * J + i, bsz, stride=P), :] for i in range(J)], axis=1)
             for c in range(n_c)], axis=0)
        utb = ut.astype(BF16)
        st_ref[:, k * k_states:(k + 1) * k_states] = jnp.dot(utb, win_ref[k], preferred_element_type=F32)
        y_ref[k] = jnp.dot(utb, wintra_ref[k], preferred_element_type=F32) + dsk_ref[k:k + 1, :] * ut

        for q in range(q_pairs):
            lr = k * k_states + q * LANES
            li = lr + k_states // 2
            ar = jnp.broadcast_to(ajr_ref[k:k + 1, q * LANES:(q + 1) * LANES], (bsz, LANES))
            ai = jnp.broadcast_to(aji_ref[k:k + 1, q * LANES:(q + 1) * LANES], (bsz, LANES))
            xr = carry_ref[:, lr:lr + LANES]
            xi = carry_ref[:, li:li + LANES]
            for c in range(n_c):
                rows = slice(c * bsz, (c + 1) * bsz)
                sr = st_ref[rows, lr:lr + LANES]
                si = st_ref[rows, li:li + LANES]
                st_ref[rows, lr:lr + LANES] = xr
                st_ref[rows, li:li + LANES] = xi
                xr, xi = ar * xr - ai * xi + sr, ar * xi + ai * xr + si
            carry_ref[:, lr:lr + LANES] = xr
            carry_ref[:, li:li + LANES] = xi

        xs = st_ref[:, k * k_states:(k + 1) * k_states].astype(BF16)
        y_ref[k] = _gelu_tanh(y_ref[k] + jnp.dot(xs, wout_ref[k], preferred_element_type=F32))

    for r in range(J):
        z = jnp.concatenate([y_ref[k, :, r * LANES:(r + 1) * LANES] for k in range(n_k)], axis=1)
        gate = jnp.dot(z.astype(BF16), wglu_ref[...], preferred_element_type=F32) + bglu_ref[...]
        o_r = z * (1.0 / (1.0 + jnp.exp(-gate)))
        for c in range(n_c):
            for k in range(n_k):
                oscr_ref[k, pl.ds(c * J + r, bsz, stride=P), :] = o_r[c * bsz:(c + 1) * bsz, k * LANES:(k + 1) * LANES]
    for bb in range(bsz):
        for k in range(n_k):
            o_ref[bb, :, k * LANES:(k + 1) * LANES] = oscr_ref[k, bb * P:bb * P + L, :].astype(o_ref.dtype)


def _s5_call(x, nw, wu, win, wintra, wout, ajr, aji, dsk, wglu, bglu):
    bsz, seq, _ = x.shape
    L, J = S5_BLOCK, S5_CHUNK
    rows_c = (L // J) * bsz
    n_k = S5_WIDTH // LANES
    const = lambda shape: pl.BlockSpec(shape, lambda t: (0,) * len(shape))
    return pl.pallas_call(
        _s5_kernel,
        grid=(seq // L,),
        in_specs=[pl.BlockSpec((bsz, L, D_MODEL), lambda t: (0, t, 0))]
        + [const(a.shape) for a in (nw, wu, win, wintra, wout, ajr, aji, dsk, wglu, bglu)],
        out_specs=pl.BlockSpec((bsz, L, S5_WIDTH), lambda t: (0, t, 0)),
        out_shape=jax.ShapeDtypeStruct((bsz, seq, S5_WIDTH), BF16),
        scratch_shapes=[
            pltpu.VMEM((n_k, bsz * S5_PITCH, LANES), F32),
            pltpu.VMEM((rows_c, 2 * S5_STATES), F32),
            pltpu.VMEM((bsz, 2 * S5_STATES), F32),
            pltpu.VMEM((n_k, rows_c, J * LANES), F32),
            pltpu.VMEM((n_k, bsz * S5_PITCH, LANES), F32),
        ],
        compiler_params=pltpu.CompilerParams(
            dimension_semantics=("arbitrary",), vmem_limit_bytes=VMEM_LIMIT_BYTES),
        name="s5_mixer",
    )(x, nw, wu, win, wintra, wout, ajr, aji, dsk, wglu, bglu)


def _s5_prepare(a_re, a_im, log_dt, b_re, b_im, c_re, c_im, d_skip):
    J = S5_CHUNK
    n_k = S5_WIDTH // LANES
    gpk = S5_GROUPS // n_k
    hi = lax.Precision.HIGHEST
    dt = jnp.exp(log_dt)[:, None]

    def apow(m):
        mag = jnp.exp(m * a_re * dt)
        return mag * jnp.cos(m * a_im * dt), mag * jnp.sin(m * a_im * dt)

    abr, abi = apow(1)
    den = a_re * a_re + a_im * a_im
    nr = abr - 1.0
    fr = (nr * a_re + abi * a_im) / den
    fi = (abi * a_re - nr * a_im) / den
    bbr = fr[..., None] * b_re - fi[..., None] * b_im
    bbi = fr[..., None] * b_im + fi[..., None] * b_re

    ab_re, ab_im = [], []
    for m in range(J):
        pr, pi = apow(m)
        ab_re.append(pr[..., None] * bbr - pi[..., None] * bbi)
        ab_im.append(pr[..., None] * bbi + pi[..., None] * bbr)

    eye = jnp.eye(gpk, dtype=F32)
    split = lambda t, axis: t.reshape(t.shape[:axis] + (n_k, gpk) + t.shape[axis + 1:])

    t_in = jnp.stack([jnp.stack([ab_re[J - 1 - i], ab_im[J - 1 - i]]) for i in range(J)])
    win = jnp.einsum('iakgnp,gh->kigpahn', split(t_in, 2), eye).reshape(n_k, J * LANES, 2 * gpk * S5_STATE)

    kmat = [jnp.einsum('gqn,gnp->gpq', c_re, ab_re[m], precision=hi)
            - jnp.einsum('gqn,gnp->gpq', c_im, ab_im[m], precision=hi) for m in range(J)]
    zero = jnp.zeros_like(kmat[0])
    t_intra = jnp.stack([jnp.stack([kmat[r - i] if i <= r else zero for r in range(J)]) for i in range(J)])
    wintra = jnp.einsum('irkgpq,gh->kigprhq', split(t_intra, 2), eye).reshape(n_k, J * LANES, J * LANES)

    t_out = []
    for r in range(J):
        pr, pi = apow(r + 1)
        t_out.append(jnp.stack([c_re * pr[:, None, :] - c_im * pi[:, None, :],
                                -(c_re * pi[:, None, :] + c_im * pr[:, None, :])]))
    wout = jnp.einsum('rakgqn,gh->kagnrhq', split(jnp.stack(t_out), 2), eye).reshape(
        n_k, 2 * gpk * S5_STATE, J * LANES)

    ajr, aji = apow(J)
    dsk = jnp.tile(d_skip.reshape(n_k, LANES), (1, J))
    return (win.astype(BF16), wintra.astype(BF16), wout.astype(BF16),
            ajr.reshape(n_k, gpk * S5_STATE), aji.reshape(n_k, gpk * S5_STATE), dsk)


def _out_mlp_kernel(x_ref, og0_ref, og1_ref, os_ref, wo_ref, nw_ref, wup_ref, wdn_ref, fw_ref, o_ref):
    og = jnp.concatenate([og0_ref[...], og1_ref[...]], axis=0)
    x1 = x_ref[...] + jnp.dot(og, wo_ref[0:GLA_WIDTH, :], preferred_element_type=F32) \
        + jnp.dot(os_ref[...], wo_ref[GLA_WIDTH:, :], preferred_element_type=F32)
    h = _rmsnorm(x1, nw_ref[...]).astype(BF16)
    acts = []
    for c in range(D_FF // MLP_FF_CHUNK):
        cs = slice(c * MLP_FF_CHUNK, (c + 1) * MLP_FF_CHUNK)
        a = jnp.maximum(jnp.dot(h, wup_ref[:, cs], preferred_element_type=F32), 0.0)
        acts.append((a * a).astype(BF16))
    x2 = x1 + jnp.dot(jnp.concatenate(acts, axis=1), wdn_ref[...], preferred_element_type=F32)
    o_ref[...] = _rmsnorm(x2, fw_ref[...])


def _out_mlp_call(x2, og_blocks, os_, wo, nw, wup, wdn, fw, seq):
    rows = x2.shape[0]
    nb, L = og_blocks.shape[1], og_blocks.shape[2]
    assert MLP_ROWS == 2 * L and seq % MLP_ROWS == 0
    tiles_per_seq = seq // MLP_ROWS
    const = lambda shape: pl.BlockSpec(shape, lambda i: (0,) * len(shape))
    tile = lambda width: pl.BlockSpec((MLP_ROWS, width), lambda i: (i, 0))

    def og_spec(e):
        def index(i):
            b, t = i // tiles_per_seq, i % tiles_per_seq
            return ((b // nb) * (seq // L) + 2 * t + e + 1, b % nb, 0, 0)
        return pl.BlockSpec((None, None, L, GLA_WIDTH), index)

    return pl.pallas_call(
        _out_mlp_kernel,
        grid=(rows // MLP_ROWS,),
        in_specs=[tile(D_MODEL), og_spec(0), og_spec(1), tile(S5_WIDTH)]
        + [const(a.shape) for a in (wo, nw, wup, wdn, fw)],
        out_specs=tile(D_MODEL),
        out_shape=jax.ShapeDtypeStruct((rows, D_MODEL), F32),
        compiler_params=pltpu.CompilerParams(
            dimension_semantics=("arbitrary",), vmem_limit_bytes=VMEM_LIMIT_BYTES),
        name="out_mlp",
    )(x2, og_blocks, og_blocks, os_, wo, nw, wup, wdn, fw)


def kernel(x, norm_mix_w, w_in, w_gk_up, b_gk, gla_norm_w, s5_a_re, s5_a_im, s5_log_dt, s5_b_re, s5_b_im,
           s5_c_re, s5_c_im, s5_d, w_glu, b_glu, w_out, norm_mlp_w, w_mlp_up, w_mlp_down, norm_final_w):
    bsz, seq, _ = x.shape
    assert bsz % GLA_SEQS == 0 and seq % GLA_BLOCK == 0 and seq % S5_BLOCK == 0 and (bsz * seq) % MLP_ROWS == 0
    assert norm_mix_w.shape[0] == 1, "single-layer problem"
    for l in range(1):
        w = w_in[l]
        kd2, r = 2 * GLA_KEY_DIM, GLA_GATE_RANK
        wg = jnp.concatenate([
            w[:, 0:kd2 + GLA_WIDTH], w[:, kd2 + GLA_WIDTH + r:kd2 + 2 * GLA_WIDTH + r],
            w[:, kd2 + GLA_WIDTH:kd2 + GLA_WIDTH + r], jnp.zeros((D_MODEL, LANES - r), F32)], axis=1).astype(BF16)
        wu = w[:, kd2 + 2 * GLA_WIDTH + r:].astype(BF16)
        wup = jnp.concatenate([w_gk_up[l], jnp.zeros((LANES - r, GLA_KEY_DIM), F32)], axis=0)
        nw = norm_mix_w[l][None, :]

        o_gla = _gla_call(x, nw, wg, wup, b_gk[l][None, :], gla_norm_w[l][None, :])

        win, wintra, wout, ajr, aji, dsk = _s5_prepare(s5_a_re[l], s5_a_im[l], s5_log_dt[l], s5_b_re[l], s5_b_im[l],
                                                       s5_c_re[l], s5_c_im[l], s5_d[l])
        o_s5 = _s5_call(x, nw, wu, win, wintra, wout, ajr, aji, dsk, w_glu[l].astype(BF16), b_glu[l][None, :])

        rows = bsz * seq
        out = _out_mlp_call(x.reshape(rows, D_MODEL), o_gla, o_s5.reshape(rows, S5_WIDTH),
                            w_out[l].astype(BF16), norm_mlp_w[l][None, :], w_mlp_up[l].astype(BF16),
                            w_mlp_down[l].astype(BF16), norm_final_w[None, :], seq)
        x = out.reshape(bsz, seq, D_MODEL)
    return x
```

```python
import functools
import math

import jax
import jax.numpy as jnp
import numpy as np
from jax import lax
from jax.experimental import pallas as pl
from jax.experimental.pallas import tpu as pltpu

F32 = jnp.float32
BF16 = jnp.bfloat16

D_MODEL = 1024
GLA_WIDTH = 512
S5_WIDTH = 512
GLA_HEADS = 4
GLA_KEY_DIM = 256
GLA_DK = 64
GLA_DV = 128
GLA_GATE_RANK = 16
GLA_GATE_NORMALIZER = 16.0
GLA_CHUNK = 64
S5_GROUP = 16
S5_GROUPS = 32
S5_STATE = 64
D_FF = 4096
EPS = 1e-6

LANES = 128
MXU_DIM = 256
S5_STATES = S5_GROUPS * S5_STATE
VMEM_LIMIT_BYTES = 56 * 1024 * 1024

GLA_BLOCK = 256
GLA_SEQS = 2
S5_BLOCK = 64
S5_CHUNK = 4
S5_PITCH = S5_BLOCK + 8
MLP_ROWS = 512
MLP_FF_CHUNK = 1024


def _rmsnorm(x, w):
    return x * lax.rsqrt(jnp.mean(x * x, axis=-1, keepdims=True) + EPS) * w


def _bf16_parts(x, n):
    parts = []
    for _ in range(n):
        part = x.astype(BF16)
        parts.append(part)
        x = x - part.astype(F32)
    return parts


def _gla_substep(x_ref, p_new_ref, p_ref, st_ref, o_ref, first, nw_ref, wg_ref, w_cat, w_lo, bgk_ref, gnw_ref,
                 causal, tri, same_head, lane_head):
    nb, L, _ = x_ref.shape
    n_chunks = L // GLA_CHUNK
    seqs = range(nb)
    cols = lambda i, c0, c1: p_ref[i * L:(i + 1) * L, c0:c1]

    hb = _rmsnorm(x_ref[...].reshape(nb * L, D_MODEL), nw_ref[...]).astype(BF16)
    n_cols = wg_ref.shape[1]
    tiles = [(c0, min(c0 + MXU_DIM, n_cols)) for c0 in range(0, n_cols, MXU_DIM)]

    def project_tile():
        if tiles:
            c0, c1 = tiles.pop(0)
            p_new_ref[:, c0:c1] = jnp.dot(hb, wg_ref[:, c0:c1], preferred_element_type=F32)

    log_a = []
    for i in seqs:
        gl_hi, gl_lo = _bf16_parts(cols(i, 1536, 1664), 2)
        gk = (jnp.dot(jnp.concatenate([gl_hi, gl_lo], axis=1), w_cat, preferred_element_type=F32)
              + jnp.dot(gl_hi, w_lo, preferred_element_type=F32) + bgk_ref[...])
        log_a.append((jnp.minimum(gk, 0.0) - jnp.log1p(jnp.exp(-jnp.abs(gk)))) * (1.0 / GLA_GATE_NORMALIZER))
    project_tile()

    b = [sum(jnp.dot(tri, part, preferred_element_type=F32) for part in _bf16_parts(log_a[i], 3)) for i in seqs]
    b_last = [jnp.concatenate(
        [jnp.broadcast_to(b[i][c * GLA_CHUNK + GLA_CHUNK - 1:(c + 1) * GLA_CHUNK, :], (GLA_CHUNK, GLA_KEY_DIM))
         for c in range(n_chunks)], axis=0) for i in seqs]
    project_tile()

    q_e = [(cols(i, 0, 256) * (GLA_DK ** -0.5)) * jnp.exp(b[i]) for i in seqs]
    k_e = [(cols(i, 256, 512) * jnp.exp(-b[i])).astype(BF16) for i in seqs]
    k_d = [(cols(i, 256, 512) * jnp.exp(b_last[i] - b[i])).astype(BF16) for i in seqs]
    v_b = [cols(i, 512, 1024).astype(BF16) for i in seqs]
    project_tile()

    o_intra = [[] for _ in seqs]
    for hh in range(GLA_HEADS):
        for i in seqs:
            q_h = jnp.where(lane_head == hh, q_e[i], 0.0).astype(BF16)
            s = lax.dot_general(q_h, k_e[i], (((1,), (1,)), ((), ())), preferred_element_type=F32)
            s = jnp.where(causal, s, 0.0).astype(BF16)
            o_intra[i].append(jnp.dot(s, v_b[i][:, hh * GLA_DV:(hh + 1) * GLA_DV], preferred_element_type=F32))
        project_tile()

    q_b = [q_e[i].astype(BF16) for i in seqs]
    st = [jnp.where(first, 0.0, st_ref[i]) for i in seqs]
    o_inter = [[] for _ in seqs]
    for c in range(n_chunks):
        r0, r1 = c * GLA_CHUNK, (c + 1) * GLA_CHUNK
        for i in seqs:
            o_inter[i].append(lax.dot_general(q_b[i][r0:r1], st[i].astype(BF16), (((1,), (1,)), ((), ())),
                                              preferred_element_type=F32))
            upd = lax.dot_general(v_b[i][r0:r1], k_d[i][r0:r1], (((0,), (0,)), ((), ())),
                                  preferred_element_type=F32)
            decay = jnp.exp(b_last[i][r0:r0 + 1, :])
            st[i] = jnp.where(same_head, st[i] * decay + upd, 0.0)
        project_tile()

    for i in seqs:
        st_ref[i] = st[i]
        o = jnp.concatenate(o_intra[i], axis=1) + jnp.concatenate(o_inter[i], axis=0)
        outs = []
        for hh in range(GLA_HEADS):
            oh = o[:, hh * GLA_DV:(hh + 1) * GLA_DV]
            outs.append(oh * lax.rsqrt(jnp.mean(oh * oh, axis=-1, keepdims=True) + EPS) * gnw_ref[...])
        g = cols(i, 1024, 1536)
        o_ref[i] = (jnp.concatenate(outs, axis=1) * (g * (1.0 / (1.0 + jnp.exp(-g))))).astype(o_ref.dtype)
    while tiles:
        project_tile()


def _gla_kernel(blocks_per_seq, xa_ref, xb_ref, nw_ref, wg_ref, wup_ref, bgk_ref, gnw_ref, o_ref,
                st_ref, pa_ref, pb_ref):
    s = pl.program_id(0)
    L = xa_ref.shape[1]

    @pl.when(s == 0)
    def _():
        st_ref[...] = jnp.zeros_like(st_ref)
        pb_ref[...] = jnp.zeros_like(pb_ref)

    w_hi, w_lo = _bf16_parts(wup_ref[...], 2)
    w_cat = jnp.concatenate([w_hi, w_hi], axis=0)
    row = lax.broadcasted_iota(jnp.int32, (L, L), 0)
    col = lax.broadcasted_iota(jnp.int32, (L, L), 1)
    causal = ((row // GLA_CHUNK) == (col // GLA_CHUNK)) & (col <= row)
    tri = jnp.where(causal, 1.0, 0.0).astype(BF16)
    srow = lax.broadcasted_iota(jnp.int32, st_ref.shape[1:], 0) // GLA_DV
    scol = lax.broadcasted_iota(jnp.int32, st_ref.shape[1:], 1) // GLA_DK
    same_head = srow == scol
    lane_head = lax.broadcasted_iota(jnp.int32, (L, GLA_KEY_DIM), 1) // GLA_DK
    shared = (nw_ref, wg_ref, w_cat, w_lo, bgk_ref, gnw_ref, causal, tri, same_head, lane_head)

    _gla_substep(xa_ref, pa_ref, pb_ref, st_ref, o_ref.at[0], (2 * s - 1) % blocks_per_seq == 0, *shared)
    _gla_substep(xb_ref, pb_ref, pa_ref, st_ref, o_ref.at[1], (2 * s) % blocks_per_seq == 0, *shared)


def _gla_call(x, nw, wg, wup, bgk, gnw):
    bsz, seq, _ = x.shape
    nb, L = GLA_SEQS, GLA_BLOCK
    bps = seq // L
    n_blocks = (bsz // nb) * bps
    assert n_blocks % 2 == 0 and bps % 2 == 0
    const = lambda shape: pl.BlockSpec(shape, lambda s: (0,) * len(shape))

    def x_spec(e):
        def index(s):
            n = jnp.minimum(2 * s + e, n_blocks - 1)
            return (n // bps, n % bps, 0)
        return pl.BlockSpec((nb, L, D_MODEL), index)

    return pl.pallas_call(
        functools.partial(_gla_kernel, bps),
        grid=(n_blocks // 2 + 1,),
        in_specs=[x_spec(0), x_spec(1),
                  const(nw.shape), const(wg.shape), const(wup.shape), const(bgk.shape), const(gnw.shape)],
        out_specs=pl.BlockSpec((2, nb, L, GLA_WIDTH), lambda s: (s, 0, 0, 0)),
        out_shape=jax.ShapeDtypeStruct((n_blocks + 2, nb, L, GLA_WIDTH), BF16),
        scratch_shapes=[pltpu.VMEM((nb, GLA_HEADS * GLA_DV, GLA_HEADS * GLA_DK), F32),
                        pltpu.VMEM((nb * L, wg.shape[1]), F32),
                        pltpu.VMEM((nb * L, wg.shape[1]), F32)],
        compiler_params=pltpu.CompilerParams(
            dimension_semantics=("arbitrary",), vmem_limit_bytes=VMEM_LIMIT_BYTES),
        name="gla_mixer",
    )(x, x, nw, wg, wup, bgk, gnw)


def _gelu_tanh(y):
    return 0.5 * y * (1.0 + jnp.tanh(np.float32(math.sqrt(2.0 / math.pi)) * (y + 0.044715 * (y * y * y))))


def _s5_kernel(x_ref, nw_ref, wu_ref, win_ref, wintra_ref, wout_ref, ajr_ref, aji_ref, dsk_ref, wglu_ref, bglu_ref,
               o_ref, u_ref, st_ref, carry_ref, y_ref, oscr_ref):
    bsz, L, _ = x_ref.shape
    J, P = S5_CHUNK, S5_PITCH
    n_c = L // J
    n_k = S5_WIDTH // LANES
    k_states = 2 * S5_STATES // n_k
    q_pairs = k_states // 2 // LANES

    @pl.when(pl.program_id(0) == 0)
    def _():
        carry_ref[...] = jnp.zeros_like(carry_ref)

    h = _rmsnorm(x_ref[...].reshape(bsz * L, D_MODEL), nw_ref[...])
    u = jnp.dot(h.astype(BF16), wu_ref[...], preferred_element_type=F32)
    for bb in range(bsz):
        for k in range(n_k):
            u_ref[k, bb * P:bb * P + L, :] = u[bb * L:(bb + 1) * L, k * LANES:(k + 1) * LANES]

    for k in range(n_k):
        ut = jnp.concatenate(
            [jnp.concatenate([u_ref[k, pl.ds(c * J + i, bsz, stride=P), :] for i in range(J)], axis=1)
             for c in range(n_c)], axis=0)
        utb = ut.astype(BF16)
        st_ref[:, k * k_states:(k + 1) * k_states] = jnp.dot(utb, win_ref[k], preferred_element_type=F32)
        y_ref[k] = jnp.dot(utb, wintra_ref[k], preferred_element_type=F32) + dsk_ref[k:k + 1, :] * ut

        for q in range(q_pairs):
            lr = k * k_states + q * LANES
            li = lr + k_states // 2
            ar = jnp.broadcast_to(ajr_ref[k:k + 1, q * LANES:(q + 1) * LANES], (bsz, LANES))
            ai = jnp.broadcast_to(aji_ref[k:k + 1, q * LANES:(q + 1) * LANES], (bsz, LANES))
            xr = carry_ref[:, lr:lr + LANES]
            xi = carry_ref[:, li:li + LANES]
            for c in range(n_c):
                rows = slice(c * bsz, (c + 1) * bsz)
                sr = st_ref[rows, lr:lr + LANES]
                si = st_ref[rows, li:li + LANES]
                st_ref[rows, lr:lr + LANES] = xr
                st_ref[rows, li:li + LANES] = xi
                xr, xi = ar * xr - ai * xi + sr, ar * xi + ai * xr + si
            carry_ref[:, lr:lr + LANES] = xr
            carry_ref[:, li:li + LANES] = xi

        xs = st_ref[:, k * k_states:(k + 1) * k_states].astype(BF16)
        y_ref[k] = _gelu_tanh(y_ref[k] + jnp.dot(xs, wout_ref[k], preferred_element_type=F32))

    for r in range(J):
        z = jnp.concatenate([y_ref[k, :, r * LANES:(r + 1) * LANES] for k in range(n_k)], axis=1)
        gate = jnp.dot(z.astype(BF16), wglu_ref[...], preferred_element_type=F32) + bglu_ref[...]
        o_r = z * (1.0 / (1.0 + jnp.exp(-gate)))
        for c in range(n_c):
            for k in range(n_k):
                oscr_ref[k, pl.ds(c * J + r, bsz, stride=P), :] = o_r[c * bsz:(c + 1) * bsz, k * LANES:(k + 1) * LANES]
    for bb in range(bsz):
        for k in range(n_k):
            o_ref[bb, :, k * LANES:(k + 1) * LANES] = oscr_ref[k, bb * P:bb * P + L, :].astype(o_ref.dtype)


def _s5_call(x, nw, wu, win, wintra, wout, ajr, aji, dsk, wglu, bglu):
    bsz, seq, _ = x.shape
    L, J = S5_BLOCK, S5_CHUNK
    rows_c = (L // J) * bsz
    n_k = S5_WIDTH // LANES
    const = lambda shape: pl.BlockSpec(shape, lambda t: (0,) * len(shape))
    return pl.pallas_call(
        _s5_kernel,
        grid=(seq // L,),
        in_specs=[pl.BlockSpec((bsz, L, D_MODEL), lambda t: (0, t, 0))]
        + [const(a.shape) for a in (nw, wu, win, wintra, wout, ajr, aji, dsk, wglu, bglu)],
        out_specs=pl.BlockSpec((bsz, L, S5_WIDTH), lambda t: (0, t, 0)),
        out_shape=jax.ShapeDtypeStruct((bsz, seq, S5_WIDTH), BF16),
        scratch_shapes=[
            pltpu.VMEM((n_k, bsz * S5_PITCH, LANES), F32),
            pltpu.VMEM((rows_c, 2 * S5_STATES), F32),
            pltpu.VMEM((bsz, 2 * S5_STATES), F32),
            pltpu.VMEM((n_k, rows_c, J * LANES), F32),
            pltpu.VMEM((n_k, bsz * S5_PITCH, LANES), F32),
        ],
        compiler_params=pltpu.CompilerParams(
            dimension_semantics=("arbitrary",), vmem_limit_bytes=VMEM_LIMIT_BYTES),
        name="s5_mixer",
    )(x, nw, wu, win, wintra, wout, ajr, aji, dsk, wglu, bglu)


def _s5_prepare(a_re, a_im, log_dt, b_re, b_im, c_re, c_im, d_skip):
    J = S5_CHUNK
    n_k = S5_WIDTH // LANES
    gpk = S5_GROUPS // n_k
    lanes = gpk * S5_STATE
    hi = lax.Precision.HIGHEST

    per_state = lambda t: t.reshape(n_k, 1, lanes)
    ar, ai = per_state(a_re), per_state(a_im)
    dt = per_state(jnp.broadcast_to(jnp.exp(log_dt)[:, None], a_re.shape))
    b_t = lambda t: t.reshape(n_k, gpk, S5_STATE, S5_GROUP).transpose(0, 3, 1, 2).reshape(n_k, S5_GROUP, lanes)
    c_t = lambda t: t.reshape(n_k, gpk, S5_GROUP, S5_STATE).transpose(0, 2, 1, 3).reshape(n_k, S5_GROUP, lanes)
    b_re, b_im, c_re, c_im = b_t(b_re), b_t(b_im), c_t(c_re), c_t(c_im)

    def apow(m):
        mag = jnp.exp(m * ar * dt)
        return mag * jnp.cos(m * ai * dt), mag * jnp.sin(m * ai * dt)

    abr, abi = apow(1)
    den = ar * ar + ai * ai
    nr = abr - 1.0
    fr = (nr * ar + abi * ai) / den
    fi = (abi * ar - nr * ai) / den
    bbr = fr * b_re - fi * b_im
    bbi = fr * b_im + fi * b_re

    ab_re, ab_im = [], []
    for m in range(J):
        pr, pi = apow(m)
        ab_re.append(pr * bbr - pi * bbi)
        ab_im.append(pr * bbi + pi * bbr)

    def block_diag(t):
        row_g = lax.broadcasted_iota(jnp.int32, (gpk, 1, 2 * lanes), 0)
        lane_g = (lax.broadcasted_iota(jnp.int32, (gpk, 1, 2 * lanes), 2) % lanes) // S5_STATE
        full = jnp.where(row_g == lane_g, t[:, :, None, :, :], 0.0)
        return full.reshape(n_k, J * LANES, 2 * lanes)

    win = block_diag(jnp.stack(
        [jnp.concatenate([ab_re[J - 1 - i], ab_im[J - 1 - i]], axis=-1) for i in range(J)], axis=1))

    t_out = []
    for r in range(J):
        pr, pi = apow(r + 1)
        t_out.append(jnp.concatenate([c_re * pr - c_im * pi, -(c_re * pi + c_im * pr)], axis=-1))
    wout = block_diag(jnp.stack(t_out, axis=1)).transpose(0, 2, 1)

    grp = lambda t: t.reshape(n_k, S5_GROUP, gpk, S5_STATE)
    kmat = [jnp.einsum('kqgn,kpgn->kgpq', grp(c_re), grp(ab_re[m]), precision=hi)
            - jnp.einsum('kqgn,kpgn->kgpq', grp(c_im), grp(ab_im[m]), precision=hi) for m in range(J)]
    zero = jnp.zeros_like(kmat[0])
    t_intra = jnp.stack([jnp.stack([kmat[r - i] if i <= r else zero for r in range(J)]) for i in range(J)])
    eye = jnp.eye(gpk, dtype=F32)
    wintra = jnp.einsum('irkgpq,gh->kigprhq', t_intra, eye).reshape(n_k, J * LANES, J * LANES)

    ajr, aji = apow(J)
    dsk = jnp.tile(d_skip.reshape(n_k, LANES), (1, J))
    return (win.astype(BF16), wintra.astype(BF16), wout.astype(BF16),
            ajr.reshape(n_k, lanes), aji.reshape(n_k, lanes), dsk)


def _out_mlp_kernel(x_ref, og0_ref, og1_ref, os_ref, wo_ref, nw_ref, wup_ref, wdn_ref, fw_ref, o_ref):
    og = jnp.concatenate([og0_ref[...], og1_ref[...]], axis=0)
    x1 = x_ref[...] + jnp.dot(og, wo_ref[0:GLA_WIDTH, :], preferred_element_type=F32) \
        + jnp.dot(os_ref[...], wo_ref[GLA_WIDTH:, :], preferred_element_type=F32)
    h = _rmsnorm(x1, nw_ref[...]).astype(BF16)
    acts = []
    for c in range(D_FF // MLP_FF_CHUNK):
        cs = slice(c * MLP_FF_CHUNK, (c + 1) * MLP_FF_CHUNK)
        a = jnp.maximum(jnp.dot(h, wup_ref[:, cs], preferred_element_type=F32), 0.0)
        acts.append((a * a).astype(BF16))
    x2 = x1 + jnp.dot(jnp.concatenate(acts, axis=1), wdn_ref[...], preferred_element_type=F32)
    o_ref[...] = _rmsnorm(x2, fw_ref[...])


def _out_mlp_call(x2, og_blocks, os_, wo, nw, wup, wdn, fw, seq):
    rows = x2.shape[0]
    nb, L = og_blocks.shape[1], og_blocks.shape[2]
    assert MLP_ROWS == 2 * L and seq % MLP_ROWS == 0
    tiles_per_seq = seq // MLP_ROWS
    const = lambda shape: pl.BlockSpec(shape, lambda i: (0,) * len(shape))
    tile = lambda width: pl.BlockSpec((MLP_ROWS, width), lambda i: (i, 0))

    def og_spec(e):
        def index(i):
            b, t = i // tiles_per_seq, i % tiles_per_seq
            return ((b // nb) * (seq // L) + 2 * t + e + 1, b % nb, 0, 0)
        return pl.BlockSpec((None, None, L, GLA_WIDTH), index)

    return pl.pallas_call(
        _out_mlp_kernel,
        grid=(rows // MLP_ROWS,),
        in_specs=[tile(D_MODEL), og_spec(0), og_spec(1), tile(S5_WIDTH)]
        + [const(a.shape) for a in (wo, nw, wup, wdn, fw)],
        out_specs=tile(D_MODEL),
        out_shape=jax.ShapeDtypeStruct((rows, D_MODEL), F32),
        compiler_params=pltpu.CompilerParams(
            dimension_semantics=("arbitrary",), vmem_limit_bytes=VMEM_LIMIT_BYTES),
        name="out_mlp",
    )(x2, og_blocks, og_blocks, os_, wo, nw, wup, wdn, fw)


def kernel(x, norm_mix_w, w_in, w_gk_up, b_gk, gla_norm_w, s5_a_re, s5_a_im, s5_log_dt, s5_b_re, s5_b_im,
           s5_c_re, s5_c_im, s5_d, w_glu, b_glu, w_out, norm_mlp_w, w_mlp_up, w_mlp_down, norm_final_w):
    bsz, seq, _ = x.shape
    assert bsz % GLA_SEQS == 0 and seq % GLA_BLOCK == 0 and seq % S5_BLOCK == 0 and (bsz * seq) % MLP_ROWS == 0
    assert norm_mix_w.shape[0] == 1, "single-layer problem"
    for l in range(1):
        w = w_in[l]
        kd2, r = 2 * GLA_KEY_DIM, GLA_GATE_RANK
        wg = jnp.concatenate([
            w[:, 0:kd2 + GLA_WIDTH], w[:, kd2 + GLA_WIDTH + r:kd2 + 2 * GLA_WIDTH + r],
            w[:, kd2 + GLA_WIDTH:kd2 + GLA_WIDTH + r], jnp.zeros((D_MODEL, LANES - r), F32)], axis=1).astype(BF16)
        wu = w[:, kd2 + 2 * GLA_WIDTH + r:].astype(BF16)
        wup = jnp.concatenate([w_gk_up[l], jnp.zeros((LANES - r, GLA_KEY_DIM), F32)], axis=0)
        nw = norm_mix_w[l][None, :]

        o_gla = _gla_call(x, nw, wg, wup, b_gk[l][None, :], gla_norm_w[l][None, :])

        win, wintra, wout, ajr, aji, dsk = _s5_prepare(s5_a_re[l], s5_a_im[l], s5_log_dt[l], s5_b_re[l], s5_b_im[l],
                                                       s5_c_re[l], s5_c_im[l], s5_d[l])
        o_s5 = _s5_call(x, nw, wu, win, wintra, wout, ajr, aji, dsk, w_glu[l].astype(BF16), b_glu[l][None, :])

        rows = bsz * seq
        out = _out_mlp_call(x.reshape(rows, D_MODEL), o_gla, o_s5.reshape(rows, S5_WIDTH),
                            w_out[l].astype(BF16), norm_mlp_w[l][None, :], w_mlp_up[l].astype(BF16),
                            w_mlp_down[l].astype(BF16), norm_final_w[None, :], seq)
        x = out.reshape(bsz, seq, D_MODEL)
    return x
```

```python
import functools
import math

import jax
import jax.numpy as jnp
import numpy as np
from jax import lax
from jax.experimental import pallas as pl
from jax.experimental.pallas import tpu as pltpu

F32 = jnp.float32
BF16 = jnp.bfloat16

D_MODEL = 1024
GLA_WIDTH = 512
S5_WIDTH = 512
GLA_HEADS = 4
GLA_KEY_DIM = 256
GLA_DK = 64
GLA_DV = 128
GLA_GATE_RANK = 16
GLA_GATE_NORMALIZER = 16.0
GLA_CHUNK = 64
S5_GROUP = 16
S5_GROUPS = 32
S5_STATE = 64
D_FF = 4096
EPS = 1e-6

LANES = 128
MXU_DIM = 256
S5_STATES = S5_GROUPS * S5_STATE
VMEM_LIMIT_BYTES = 56 * 1024 * 1024

GLA_BLOCK = 256
GLA_SEQS = 2
S5_BLOCK = 64
S5_CHUNK = 4
S5_PITCH = S5_BLOCK + 8
MLP_ROWS = 512
MLP_FF_CHUNK = 1024


def _rmsnorm(x, w):
    return x * lax.rsqrt(jnp.mean(x * x, axis=-1, keepdims=True) + EPS) * w


def _bf16_parts(x, n):
    parts = []
    for _ in range(n):
        part = x.astype(BF16)
        parts.append(part)
        x = x - part.astype(F32)
    return parts


def _gla_substep(x_ref, p_new_ref, p_ref, st_ref, o_ref, first, nw_ref, wg_ref, w_cat, w_lo, bgk_ref, gnw_ref,
                 causal, tri, same_head, lane_head, chunk_of_t):
    nb, L, _ = x_ref.shape
    n_chunks = L // GLA_CHUNK
    seqs = range(nb)
    cols = lambda i, c0, c1: p_ref[i * L:(i + 1) * L, c0:c1]

    hb = _rmsnorm(x_ref[...].reshape(nb * L, D_MODEL), nw_ref[...]).astype(BF16)
    n_cols = wg_ref.shape[1]
    tiles = [(c0, min(c0 + MXU_DIM, n_cols)) for c0 in range(0, n_cols, MXU_DIM)]

    def project_tile():
        if tiles:
            c0, c1 = tiles.pop(0)
            p_new_ref[:, c0:c1] = jnp.dot(hb, wg_ref[:, c0:c1], preferred_element_type=F32)

    log_a = []
    for i in seqs:
        gl_hi, gl_lo = _bf16_parts(cols(i, 1536, 1664), 2)
        gk = (jnp.dot(jnp.concatenate([gl_hi, gl_lo], axis=1), w_cat, preferred_element_type=F32)
              + jnp.dot(gl_hi, w_lo, preferred_element_type=F32) + bgk_ref[...])
        log_a.append((jnp.minimum(gk, 0.0) - jnp.log1p(jnp.exp(-jnp.abs(gk)))) * (1.0 / GLA_GATE_NORMALIZER))
    project_tile()

    b = []
    for i in seqs:
        terms = jnp.dot(tri, jnp.concatenate(_bf16_parts(log_a[i], 3), axis=1), preferred_element_type=F32)
        b.append(sum(terms[:, j * GLA_KEY_DIM:(j + 1) * GLA_KEY_DIM] for j in range(3)))
    b_last = [jnp.concatenate(
        [jnp.broadcast_to(b[i][c * GLA_CHUNK + GLA_CHUNK - 1:(c + 1) * GLA_CHUNK, :], (GLA_CHUNK, GLA_KEY_DIM))
         for c in range(n_chunks)], axis=0) for i in seqs]
    project_tile()

    q_e = [(cols(i, 0, 256) * (GLA_DK ** -0.5)) * jnp.exp(b[i]) for i in seqs]
    k_et = [(cols(i, 256, 512) * jnp.exp(-b[i])).T.astype(BF16) for i in seqs]
    k_dt = [(cols(i, 256, 512) * jnp.exp(b_last[i] - b[i])).T.astype(BF16) for i in seqs]
    b_t = [b[i].T for i in seqs]
    v_b = [cols(i, 512, 1024).astype(BF16) for i in seqs]
    project_tile()

    scores = []
    for i in seqs:
        q_heads = jnp.concatenate([jnp.where(lane_head == hh, q_e[i], 0.0) for hh in range(GLA_HEADS)], axis=0)
        scores.append(jnp.dot(q_heads.astype(BF16), k_et[i], preferred_element_type=F32))
    project_tile()
    o_intra = [[] for _ in seqs]
    for hh in range(GLA_HEADS):
        for i in seqs:
            s = jnp.where(causal, scores[i][hh * L:(hh + 1) * L], 0.0).astype(BF16)
            o_intra[i].append(jnp.dot(s, v_b[i][:, hh * GLA_DV:(hh + 1) * GLA_DV], preferred_element_type=F32))
        project_tile()

    q_b = [q_e[i].astype(BF16) for i in seqs]
    st = [jnp.where(first, 0.0, st_ref[i]) for i in seqs]
    o_inter = [[] for _ in seqs]
    for c in range(n_chunks):
        r0, r1 = c * GLA_CHUNK, (c + 1) * GLA_CHUNK
        for i in seqs:
            o_inter[i].append(jnp.dot(q_b[i][r0:r1], st[i].astype(BF16), preferred_element_type=F32))
            k_c = jnp.where(chunk_of_t == c, k_dt[i], jnp.zeros_like(k_dt[i]))
            upd = jnp.dot(k_c, v_b[i], preferred_element_type=F32)
            decay = jnp.exp(b_t[i][:, r1 - 1:r1])
            st[i] = jnp.where(same_head, st[i] * decay + upd, 0.0)
        project_tile()

    for i in seqs:
        st_ref[i] = st[i]
        o = jnp.concatenate(o_intra[i], axis=1) + jnp.concatenate(o_inter[i], axis=0)
        outs = []
        for hh in range(GLA_HEADS):
            oh = o[:, hh * GLA_DV:(hh + 1) * GLA_DV]
            outs.append(oh * lax.rsqrt(jnp.mean(oh * oh, axis=-1, keepdims=True) + EPS) * gnw_ref[...])
        g = cols(i, 1024, 1536)
        o_ref[i] = (jnp.concatenate(outs, axis=1) * (g * (1.0 / (1.0 + jnp.exp(-g))))).astype(o_ref.dtype)
    while tiles:
        project_tile()


def _gla_kernel(blocks_per_seq, xa_ref, xb_ref, nw_ref, wg_ref, wup_ref, bgk_ref, gnw_ref, o_ref,
                st_ref, pa_ref, pb_ref):
    s = pl.program_id(0)
    L = xa_ref.shape[1]

    @pl.when(s == 0)
    def _():
        st_ref[...] = jnp.zeros_like(st_ref)
        pb_ref[...] = jnp.zeros_like(pb_ref)

    w_hi, w_lo = _bf16_parts(wup_ref[...], 2)
    w_cat = jnp.concatenate([w_hi, w_hi], axis=0)
    row = lax.broadcasted_iota(jnp.int32, (L, L), 0)
    col = lax.broadcasted_iota(jnp.int32, (L, L), 1)
    causal = ((row // GLA_CHUNK) == (col // GLA_CHUNK)) & (col <= row)
    tri = jnp.where(causal, 1.0, 0.0).astype(BF16)
    srow = lax.broadcasted_iota(jnp.int32, st_ref.shape[1:], 0) // GLA_DK
    scol = lax.broadcasted_iota(jnp.int32, st_ref.shape[1:], 1) // GLA_DV
    same_head = srow == scol
    lane_head = lax.broadcasted_iota(jnp.int32, (L, GLA_KEY_DIM), 1) // GLA_DK
    chunk_of_t = lax.broadcasted_iota(jnp.int32, (GLA_KEY_DIM, L), 1) // GLA_CHUNK
    shared = (nw_ref, wg_ref, w_cat, w_lo, bgk_ref, gnw_ref, causal, tri, same_head, lane_head, chunk_of_t)

    _gla_substep(xa_ref, pa_ref, pb_ref, st_ref, o_ref.at[0], (2 * s - 1) % blocks_per_seq == 0, *shared)
    _gla_substep(xb_ref, pb_ref, pa_ref, st_ref, o_ref.at[1], (2 * s) % blocks_per_seq == 0, *shared)


def _gla_call(x, nw, wg, wup, bgk, gnw):
    bsz, seq, _ = x.shape
    nb, L = GLA_SEQS, GLA_BLOCK
    bps = seq // L
    n_blocks = (bsz // nb) * bps
    assert n_blocks % 2 == 0 and bps % 2 == 0
    const = lambda shape: pl.BlockSpec(shape, lambda s: (0,) * len(shape))

    def x_spec(e):
        def index(s):
            n = jnp.minimum(2 * s + e, n_blocks - 1)
            return (n // bps, n % bps, 0)
        return pl.BlockSpec((nb, L, D_MODEL), index)

    return pl.pallas_call(
        functools.partial(_gla_kernel, bps),
        grid=(n_blocks // 2 + 1,),
        in_specs=[x_spec(0), x_spec(1),
                  const(nw.shape), const(wg.shape), const(wup.shape), const(bgk.shape), const(gnw.shape)],
        out_specs=pl.BlockSpec((2, nb, L, GLA_WIDTH), lambda s: (s, 0, 0, 0)),
        out_shape=jax.ShapeDtypeStruct((n_blocks + 2, nb, L, GLA_WIDTH), BF16),
        scratch_shapes=[pltpu.VMEM((nb, GLA_HEADS * GLA_DK, GLA_HEADS * GLA_DV), F32),
                        pltpu.VMEM((nb * L, wg.shape[1]), F32),
                        pltpu.VMEM((nb * L, wg.shape[1]), F32)],
        compiler_params=pltpu.CompilerParams(
            dimension_semantics=("arbitrary",), vmem_limit_bytes=VMEM_LIMIT_BYTES),
        name="gla_mixer",
    )(x, x, nw, wg, wup, bgk, gnw)


def _gelu_tanh(y):
    return 0.5 * y * (1.0 + jnp.tanh(np.float32(math.sqrt(2.0 / math.pi)) * (y + 0.044715 * (y * y * y))))


def _s5_kernel(x_ref, nw_ref, wu_ref, win_ref, wintra_ref, wout_ref, ajr_ref, aji_ref, dsk_ref, wglu_ref, bglu_ref,
               o_ref, u_ref, st_ref, carry_ref, y_ref, oscr_ref):
    bsz, L, _ = x_ref.shape
    J, P = S5_CHUNK, S5_PITCH
    n_c = L // J
    n_k = S5_WIDTH // LANES
    k_states = 2 * S5_STATES // n_k
    q_pairs = k_states // 2 // LANES

    @pl.when(pl.program_id(0) == 0)
    def _():
        carry_ref[...] = jnp.zeros_like(carry_ref)

    h = _rmsnorm(x_ref[...].reshape(bsz * L, D_MODEL), nw_ref[...])
    u = jnp.dot(h.astype(BF16), wu_ref[...], preferred_element_type=F32)
    for bb in range(bsz):
        for k in range(n_k):
            u_ref[k, bb * P:bb * P + L, :] = u[bb * L:(bb + 1) * L, k * LANES:(k + 1) * LANES]

    for k in range(n_k):
        ut = jnp.concatenate(
            [jnp.concatenate([u_ref[k, pl.ds(c * J + i, bsz, stride=P), :] for i in range(J)], axis=1)
             for c in range(n_c)], axis=0)
        utb = ut.astype(BF16)
        st_ref[:, k * k_states:(k + 1) * k_states] = jnp.dot(utb, win_ref[k], preferred_element_type=F32)
        y_ref[k] = jnp.dot(utb, wintra_ref[k], preferred_element_type=F32) + dsk_ref[k:k + 1, :] * ut

        for q in range(q_pairs):
            lr = k * k_states + q * LANES
            li = lr + k_states // 2
            ar = jnp.broadcast_to(ajr_ref[k:k + 1, q * LANES:(q + 1) * LANES], (bsz, LANES))
            ai = jnp.broadcast_to(aji_ref[k:k + 1, q * LANES:(q + 1) * LANES], (bsz, LANES))
            xr = carry_ref[:, lr:lr + LANES]
            xi = carry_ref[:, li:li + LANES]
            for c in range(n_c):
                rows = slice(c * bsz, (c + 1) * bsz)
                sr = st_ref[rows, lr:lr + LANES]
                si = st_ref[rows, li:li + LANES]
                st_ref[rows, lr:lr + LANES] = xr
                st_ref[rows, li:li + LANES] = xi
                xr, xi = ar * xr - ai * xi + sr, ar * xi + ai * xr + si
            carry_ref[:, lr:lr + LANES] = xr
            carry_ref[:, li:li + LANES] = xi

        xs = st_ref[:, k * k_states:(k + 1) * k_states].astype(BF16)
        y_ref[k] = _gelu_tanh(y_ref[k] + jnp.dot(xs, wout_ref[k], preferred_element_type=F32))

    for r in range(J):
        z = jnp.concatenate([y_ref[k, :, r * LANES:(r + 1) * LANES] for k in range(n_k)], axis=1)
        gate = jnp.dot(z.astype(BF16), wglu_ref[...], preferred_element_type=F32) + bglu_ref[...]
        o_r = z * (1.0 / (1.0 + jnp.exp(-gate)))
        for c in range(n_c):
            for k in range(n_k):
                oscr_ref[k, pl.ds(c * J + r, bsz, stride=P), :] = o_r[c * bsz:(c + 1) * bsz, k * LANES:(k + 1) * LANES]
    for bb in range(bsz):
        for k in range(n_k):
            o_ref[bb, :, k * LANES:(k + 1) * LANES] = oscr_ref[k, bb * P:bb * P + L, :].astype(o_ref.dtype)


def _s5_call(x, nw, wu, win, wintra, wout, ajr, aji, dsk, wglu, bglu):
    bsz, seq, _ = x.shape
    L, J = S5_BLOCK, S5_CHUNK
    rows_c = (L // J) * bsz
    n_k = S5_WIDTH // LANES
    const = lambda shape: pl.BlockSpec(shape, lambda t: (0,) * len(shape))
    return pl.pallas_call(
        _s5_kernel,
        grid=(seq // L,),
        in_specs=[pl.BlockSpec((bsz, L, D_MODEL), lambda t: (0, t, 0))]
        + [const(a.shape) for a in (nw, wu, win, wintra, wout, ajr, aji, dsk, wglu, bglu)],
        out_specs=pl.BlockSpec((bsz, L, S5_WIDTH), lambda t: (0, t, 0)),
        out_shape=jax.ShapeDtypeStruct((bsz, seq, S5_WIDTH), BF16),
        scratch_shapes=[
            pltpu.VMEM((n_k, bsz * S5_PITCH, LANES), F32),
            pltpu.VMEM((rows_c, 2 * S5_STATES), F32),
            pltpu.VMEM((bsz, 2 * S5_STATES), F32),
            pltpu.VMEM((n_k, rows_c, J * LANES), F32),
            pltpu.VMEM((n_k, bsz * S5_PITCH, LANES), F32),
        ],
        compiler_params=pltpu.CompilerParams(
            dimension_semantics=("arbitrary",), vmem_limit_bytes=VMEM_LIMIT_BYTES),
        name="s5_mixer",
    )(x, nw, wu, win, wintra, wout, ajr, aji, dsk, wglu, bglu)


def _s5_prepare(a_re, a_im, log_dt, b_re, b_im, c_re, c_im, d_skip):
    J = S5_CHUNK
    n_k = S5_WIDTH // LANES
    gpk = S5_GROUPS // n_k
    lanes = gpk * S5_STATE
    hi = lax.Precision.HIGHEST

    per_state = lambda t: t.reshape(n_k, 1, lanes)
    ar, ai = per_state(a_re), per_state(a_im)
    dt = per_state(jnp.broadcast_to(jnp.exp(log_dt)[:, None], a_re.shape))
    b_t = lambda t: t.reshape(n_k, gpk, S5_STATE, S5_GROUP).transpose(0, 3, 1, 2).reshape(n_k, S5_GROUP, lanes)
    c_t = lambda t: t.reshape(n_k, gpk, S5_GROUP, S5_STATE).transpose(0, 2, 1, 3).reshape(n_k, S5_GROUP, lanes)
    b_re, b_im, c_re, c_im = b_t(b_re), b_t(b_im), c_t(c_re), c_t(c_im)

    def apow(m):
        mag = jnp.exp(m * ar * dt)
        return mag * jnp.cos(m * ai * dt), mag * jnp.sin(m * ai * dt)

    abr, abi = apow(1)
    den = ar * ar + ai * ai
    nr = abr - 1.0
    fr = (nr * ar + abi * ai) / den
    fi = (abi * ar - nr * ai) / den
    bbr = fr * b_re - fi * b_im
    bbi = fr * b_im + fi * b_re

    ab_re, ab_im = [], []
    for m in range(J):
        pr, pi = apow(m)
        ab_re.append(pr * bbr - pi * bbi)
        ab_im.append(pr * bbi + pi * bbr)

    def block_diag(t, lane_period, lane_group_width):
        n_lanes = t.shape[-1]
        row_g = lax.broadcasted_iota(jnp.int32, (gpk, 1, n_lanes), 0)
        lane_g = (lax.broadcasted_iota(jnp.int32, (gpk, 1, n_lanes), 2) % lane_period) // lane_group_width
        full = jnp.where(row_g == lane_g, t[:, :, None, :, :], 0.0)
        return full.reshape(n_k, J * LANES, n_lanes)

    win = block_diag(jnp.stack(
        [jnp.concatenate([ab_re[J - 1 - i], ab_im[J - 1 - i]], axis=-1) for i in range(J)], axis=1),
        lanes, S5_STATE)

    t_out = []
    for r in range(J):
        pr, pi = apow(r + 1)
        t_out.append(jnp.concatenate([c_re * pr - c_im * pi, -(c_re * pi + c_im * pr)], axis=-1))
    wout = block_diag(jnp.stack(t_out, axis=1), lanes, S5_STATE).transpose(0, 2, 1)

    grp = lambda t: t.reshape(n_k, S5_GROUP, gpk, S5_STATE)
    kmat = [(jnp.einsum('kqgn,kpgn->kpgq', grp(c_re), grp(ab_re[m]), precision=hi)
             - jnp.einsum('kqgn,kpgn->kpgq', grp(c_im), grp(ab_im[m]), precision=hi)).reshape(n_k, S5_GROUP, LANES)
            for m in range(J)]
    zero = jnp.zeros_like(kmat[0])
    t_intra = jnp.stack([jnp.concatenate([kmat[r - i] if i <= r else zero for r in range(J)], axis=-1)
                         for i in range(J)], axis=1)
    wintra = block_diag(t_intra, LANES, S5_GROUP)

    ajr, aji = apow(J)
    dsk = jnp.tile(d_skip.reshape(n_k, LANES), (1, J))
    return (win.astype(BF16), wintra.astype(BF16), wout.astype(BF16),
            ajr.reshape(n_k, lanes), aji.reshape(n_k, lanes), dsk)


def _out_mlp_kernel(x_ref, og0_ref, og1_ref, os_ref, wo_ref, nw_ref, wup_ref, wdn_ref, fw_ref, o_ref):
    og = jnp.concatenate([og0_ref[...], og1_ref[...]], axis=0)
    x1 = x_ref[...] + jnp.dot(og, wo_ref[0:GLA_WIDTH, :], preferred_element_type=F32) \
        + jnp.dot(os_ref[...], wo_ref[GLA_WIDTH:, :], preferred_element_type=F32)
    h = _rmsnorm(x1, nw_ref[...]).astype(BF16)
    acts = []
    for c in range(D_FF // MLP_FF_CHUNK):
        cs = slice(c * MLP_FF_CHUNK, (c + 1) * MLP_FF_CHUNK)
        a = jnp.maximum(jnp.dot(h, wup_ref[:, cs], preferred_element_type=F32), 0.0)
        acts.append((a * a).astype(BF16))
    x2 = x1 + jnp.dot(jnp.concatenate(acts, axis=1), wdn_ref[...], preferred_element_type=F32)
    o_ref[...] = _rmsnorm(x2, fw_ref[...])


def _out_mlp_call(x2, og_blocks, os_, wo, nw, wup, wdn, fw, seq):
    rows = x2.shape[0]
    nb, L = og_blocks.shape[1], og_blocks.shape[2]
    assert MLP_ROWS == 2 * L and seq % MLP_ROWS == 0
    tiles_per_seq = seq // MLP_ROWS
    const = lambda shape: pl.BlockSpec(shape, lambda i: (0,) * len(shape))
    tile = lambda width: pl.BlockSpec((MLP_ROWS, width), lambda i: (i, 0))

    def og_spec(e):
        def index(i):
            b, t = i // tiles_per_seq, i % tiles_per_seq
            return ((b // nb) * (seq // L) + 2 * t + e + 1, b % nb, 0, 0)
        return pl.BlockSpec((None, None, L, GLA_WIDTH), index)

    return pl.pallas_call(
        _out_mlp_kernel,
        grid=(rows // MLP_ROWS,),
        in_specs=[tile(D_MODEL), og_spec(0), og_spec(1), tile(S5_WIDTH)]
        + [const(a.shape) for a in (wo, nw, wup, wdn, fw)],
        out_specs=tile(D_MODEL),
        out_shape=jax.ShapeDtypeStruct((rows, D_MODEL), F32),
        compiler_params=pltpu.CompilerParams(
            dimension_semantics=("arbitrary",), vmem_limit_bytes=VMEM_LIMIT_BYTES),
        name="out_mlp",
    )(x2, og_blocks, og_blocks, os_, wo, nw, wup, wdn, fw)


def kernel(x, norm_mix_w, w_in, w_gk_up, b_gk, gla_norm_w, s5_a_re, s5_a_im, s5_log_dt, s5_b_re, s5_b_im,
           s5_c_re, s5_c_im, s5_d, w_glu, b_glu, w_out, norm_mlp_w, w_mlp_up, w_mlp_down, norm_final_w):
    bsz, seq, _ = x.shape
    assert bsz % GLA_SEQS == 0 and seq % GLA_BLOCK == 0 and seq % S5_BLOCK == 0 and (bsz * seq) % MLP_ROWS == 0
    assert norm_mix_w.shape[0] == 1, "single-layer problem"
    for l in range(1):
        w = w_in[l]
        kd2, r = 2 * GLA_KEY_DIM, GLA_GATE_RANK
        wg = jnp.concatenate([
            w[:, 0:kd2 + GLA_WIDTH], w[:, kd2 + GLA_WIDTH + r:kd2 + 2 * GLA_WIDTH + r],
            w[:, kd2 + GLA_WIDTH:kd2 + GLA_WIDTH + r], jnp.zeros((D_MODEL, LANES - r), F32)], axis=1).astype(BF16)
        wu = w[:, kd2 + 2 * GLA_WIDTH + r:].astype(BF16)
        wup = jnp.concatenate([w_gk_up[l], jnp.zeros((LANES - r, GLA_KEY_DIM), F32)], axis=0)
        nw = norm_mix_w[l][None, :]

        o_gla = _gla_call(x, nw, wg, wup, b_gk[l][None, :], gla_norm_w[l][None, :])

        win, wintra, wout, ajr, aji, dsk = _s5_prepare(s5_a_re[l], s5_a_im[l], s5_log_dt[l], s5_b_re[l], s5_b_im[l],
                                                       s5_c_re[l], s5_c_im[l], s5_d[l])
        o_s5 = _s5_call(x, nw, wu, win, wintra, wout, ajr, aji, dsk, w_glu[l].astype(BF16), b_glu[l][None, :])

        rows = bsz * seq
        out = _out_mlp_call(x.reshape(rows, D_MODEL), o_gla, o_s5.reshape(rows, S5_WIDTH),
                            w_out[l].astype(BF16), norm_mlp_w[l][None, :], w_mlp_up[l].astype(BF16),
                            w_mlp_down[l].astype(BF16), norm_final_w[None, :], seq)
        x = out.reshape(bsz, seq, D_MODEL)
    return x
```

```python
import functools
import math

import jax
import jax.numpy as jnp
import numpy as np
from jax import lax
from jax.experimental import pallas as pl
from jax.experimental.pallas import tpu as pltpu

F32 = jnp.float32
BF16 = jnp.bfloat16

D_MODEL = 1024
GLA_WIDTH = 512
S5_WIDTH = 512
GLA_HEADS = 4
GLA_KEY_DIM = 256
GLA_DK = 64
GLA_DV = 128
GLA_GATE_RANK = 16
GLA_GATE_NORMALIZER = 16.0
GLA_CHUNK = 64
GLA_SPAN = 2 * GLA_CHUNK
S5_GROUP = 16
S5_GROUPS = 32
S5_STATE = 64
D_FF = 4096
EPS = 1e-6

LANES = 128
MXU_DIM = 256
S5_STATES = S5_GROUPS * S5_STATE
VMEM_LIMIT_BYTES = 56 * 1024 * 1024

GLA_BLOCK = 256
GLA_SEQS = 2
S5_BLOCK = 64
S5_CHUNK = 4
S5_PITCH = S5_BLOCK + 8
MLP_ROWS = 512
MLP_FF_CHUNK = 1024


def _rms_scale(x):
    return x * lax.rsqrt(jnp.mean(x * x, axis=-1, keepdims=True) + EPS)


def _rmsnorm(x, w):
    return _rms_scale(x) * w


def _bf16_parts(x, n):
    parts = []
    for _ in range(n):
        part = x.astype(BF16)
        parts.append(part)
        x = x - part.astype(F32)
    return parts


def _gla_substep(x_ref, p_new_ref, p_ref, st_ref, o_ref, first, wg_ref, w_cat, w_lo, bgk_ref, gnw_ref,
                 causal, tri, same_head, lane_head, span_of_t):
    nb, L, _ = x_ref.shape
    n_spans = L // GLA_SPAN
    seqs = range(nb)
    cols = lambda i, c0, c1: p_ref[i * L:(i + 1) * L, c0:c1]

    hb = _rms_scale(x_ref[...].reshape(nb * L, D_MODEL)).astype(BF16)
    n_cols = wg_ref.shape[1]
    tiles = [(c0, min(c0 + MXU_DIM, n_cols)) for c0 in range(0, n_cols, MXU_DIM)]

    def project_tile():
        if tiles:
            c0, c1 = tiles.pop(0)
            p_new_ref[:, c0:c1] = jnp.dot(hb, wg_ref[:, c0:c1], preferred_element_type=F32)

    log_a = []
    for i in seqs:
        gl_hi, gl_lo = _bf16_parts(cols(i, 1536, 1664), 2)
        gk = (jnp.dot(jnp.concatenate([gl_hi, gl_lo], axis=1), w_cat, preferred_element_type=F32)
              + jnp.dot(gl_hi, w_lo, preferred_element_type=F32) + bgk_ref[...])
        log_a.append((jnp.minimum(gk, 0.0) - jnp.log1p(jnp.exp(-jnp.abs(gk)))) * (1.0 / GLA_GATE_NORMALIZER))
    project_tile()

    b = []
    for i in seqs:
        terms = jnp.dot(tri, jnp.concatenate(_bf16_parts(log_a[i], 3), axis=1), preferred_element_type=F32)
        b.append(sum(terms[:, j * GLA_KEY_DIM:(j + 1) * GLA_KEY_DIM] for j in range(3)))
    span_row = lambda t, r: jnp.concatenate(
        [jnp.broadcast_to(t[m * GLA_SPAN + r:m * GLA_SPAN + r + 1, :], (GLA_SPAN, GLA_KEY_DIM))
         for m in range(n_spans)], axis=0)
    b_mid = [span_row(b[i], GLA_CHUNK - 1) for i in seqs]
    b_last = [span_row(b[i], GLA_SPAN - 1) for i in seqs]
    project_tile()

    q = [cols(i, 0, 256) * (GLA_DK ** -0.5) for i in seqs]
    q_s = [q[i] * jnp.exp(b[i] - b_mid[i]) for i in seqs]
    q_b = [(q[i] * jnp.exp(b[i])).astype(BF16) for i in seqs]
    k_et = [(cols(i, 256, 512) * jnp.exp(b_mid[i] - b[i])).T.astype(BF16) for i in seqs]
    k_dt = [(cols(i, 256, 512) * jnp.exp(b_last[i] - b[i])).T.astype(BF16) for i in seqs]
    b_t = [b[i].T for i in seqs]
    v_b = [cols(i, 512, 1024).astype(BF16) for i in seqs]
    project_tile()

    scores = []
    for i in seqs:
        q_heads = jnp.concatenate([jnp.where(lane_head == hh, q_s[i], 0.0) for hh in range(GLA_HEADS)], axis=0)
        scores.append(jnp.dot(q_heads.astype(BF16), k_et[i], preferred_element_type=F32))
    project_tile()
    o_intra = [[] for _ in seqs]
    for hh in range(GLA_HEADS):
        for i in seqs:
            s = jnp.where(causal, scores[i][hh * L:(hh + 1) * L], 0.0).astype(BF16)
            o_intra[i].append(jnp.dot(s, v_b[i][:, hh * GLA_DV:(hh + 1) * GLA_DV], preferred_element_type=F32))
        project_tile()

    st = [jnp.where(first, 0.0, st_ref[i]) for i in seqs]
    o_inter = [[] for _ in seqs]
    for m in range(n_spans):
        r0, r1 = m * GLA_SPAN, (m + 1) * GLA_SPAN
        for i in seqs:
            o_inter[i].append(jnp.dot(q_b[i][r0:r1], st[i].astype(BF16), preferred_element_type=F32))
            k_m = jnp.where(span_of_t == m, k_dt[i], jnp.zeros_like(k_dt[i]))
            upd = jnp.dot(k_m, v_b[i], preferred_element_type=F32)
            decay = jnp.exp(b_t[i][:, r1 - 1:r1])
            st[i] = jnp.where(same_head, st[i] * decay + upd, 0.0)
        project_tile()
        project_tile()

    for i in seqs:
        st_ref[i] = st[i]
        o = jnp.concatenate(o_intra[i], axis=1) + jnp.concatenate(o_inter[i], axis=0)
        outs = []
        for hh in range(GLA_HEADS):
            oh = o[:, hh * GLA_DV:(hh + 1) * GLA_DV]
            outs.append(oh * lax.rsqrt(jnp.mean(oh * oh, axis=-1, keepdims=True) + EPS) * gnw_ref[...])
        g = cols(i, 1024, 1536)
        o_ref[i] = (jnp.concatenate(outs, axis=1) * (g * (1.0 / (1.0 + jnp.exp(-g))))).astype(o_ref.dtype)
    while tiles:
        project_tile()


def _gla_kernel(blocks_per_seq, xa_ref, xb_ref, wg_ref, wup_ref, bgk_ref, gnw_ref, o_ref,
                st_ref, pa_ref, pb_ref):
    s = pl.program_id(0)
    L = xa_ref.shape[1]

    @pl.when(s == 0)
    def _():
        st_ref[...] = jnp.zeros_like(st_ref)
        pb_ref[...] = jnp.zeros_like(pb_ref)

    w_hi, w_lo = _bf16_parts(wup_ref[...], 2)
    w_cat = jnp.concatenate([w_hi, w_hi], axis=0)
    row = lax.broadcasted_iota(jnp.int32, (L, L), 0)
    col = lax.broadcasted_iota(jnp.int32, (L, L), 1)
    causal = ((row // GLA_SPAN) == (col // GLA_SPAN)) & (col <= row)
    tri = jnp.where(causal, 1.0, 0.0).astype(BF16)
    srow = lax.broadcasted_iota(jnp.int32, st_ref.shape[1:], 0) // GLA_DK
    scol = lax.broadcasted_iota(jnp.int32, st_ref.shape[1:], 1) // GLA_DV
    same_head = srow == scol
    lane_head = lax.broadcasted_iota(jnp.int32, (L, GLA_KEY_DIM), 1) // GLA_DK
    span_of_t = lax.broadcasted_iota(jnp.int32, (GLA_KEY_DIM, L), 1) // GLA_SPAN
    shared = (wg_ref, w_cat, w_lo, bgk_ref, gnw_ref, causal, tri, same_head, lane_head, span_of_t)

    _gla_substep(xa_ref, pa_ref, pb_ref, st_ref, o_ref.at[0], (2 * s - 1) % blocks_per_seq == 0, *shared)
    _gla_substep(xb_ref, pb_ref, pa_ref, st_ref, o_ref.at[1], (2 * s) % blocks_per_seq == 0, *shared)


def _gla_call(x, wg, wup, bgk, gnw):
    bsz, seq, _ = x.shape
    nb, L = GLA_SEQS, GLA_BLOCK
    bps = seq // L
    n_blocks = (bsz // nb) * bps
    assert n_blocks % 2 == 0 and bps % 2 == 0
    const = lambda shape: pl.BlockSpec(shape, lambda s: (0,) * len(shape))

    def x_spec(e):
        def index(s):
            n = jnp.minimum(2 * s + e, n_blocks - 1)
            return (n // bps, n % bps, 0)
        return pl.BlockSpec((nb, L, D_MODEL), index)

    return pl.pallas_call(
        functools.partial(_gla_kernel, bps),
        grid=(n_blocks // 2 + 1,),
        in_specs=[x_spec(0), x_spec(1), const(wg.shape), const(wup.shape), const(bgk.shape), const(gnw.shape)],
        out_specs=pl.BlockSpec((2, nb, L, GLA_WIDTH), lambda s: (s, 0, 0, 0)),
        out_shape=jax.ShapeDtypeStruct((n_blocks + 2, nb, L, GLA_WIDTH), BF16),
        scratch_shapes=[pltpu.VMEM((nb, GLA_HEADS * GLA_DK, GLA_HEADS * GLA_DV), F32),
                        pltpu.VMEM((nb * L, wg.shape[1]), F32),
                        pltpu.VMEM((nb * L, wg.shape[1]), F32)],
        compiler_params=pltpu.CompilerParams(
            dimension_semantics=("arbitrary",), vmem_limit_bytes=VMEM_LIMIT_BYTES),
        name="gla_mixer",
    )(x, x, wg, wup, bgk, gnw)


def _gelu_tanh(y):
    return 0.5 * y * (1.0 + jnp.tanh(np.float32(math.sqrt(2.0 / math.pi)) * (y + 0.044715 * (y * y * y))))


def _s5_kernel(x_ref, wu_ref, win_ref, wintra_ref, wout_ref, ajr_ref, aji_ref, wglu_ref, bglu_ref,
               o_ref, u_ref, st_ref, carry_ref, y_ref, oscr_ref):
    bsz, L, _ = x_ref.shape
    J, P = S5_CHUNK, S5_PITCH
    n_c = L // J
    n_k = S5_WIDTH // LANES
    k_states = 2 * S5_STATES // n_k
    q_pairs = k_states // 2 // LANES

    @pl.when(pl.program_id(0) == 0)
    def _():
        carry_ref[...] = jnp.zeros_like(carry_ref)

    h = _rms_scale(x_ref[...].reshape(bsz * L, D_MODEL))
    u = jnp.dot(h.astype(BF16), wu_ref[...], preferred_element_type=F32)
    for bb in range(bsz):
        for k in range(n_k):
            u_ref[k, bb * P:bb * P + L, :] = u[bb * L:(bb + 1) * L, k * LANES:(k + 1) * LANES]

    for k in range(n_k):
        utb = jnp.concatenate(
            [jnp.concatenate([u_ref[k, pl.ds(c * J + i, bsz, stride=P), :] for i in range(J)], axis=1)
             for c in range(n_c)], axis=0).astype(BF16)
        st_ref[:, k * k_states:(k + 1) * k_states] = jnp.dot(utb, win_ref[k], preferred_element_type=F32)
        y_ref[k] = jnp.dot(utb, wintra_ref[k], preferred_element_type=F32)

        for q in range(q_pairs):
            lr = k * k_states + q * LANES
            li = lr + k_states // 2
            ar = jnp.broadcast_to(ajr_ref[k:k + 1, q * LANES:(q + 1) * LANES], (bsz, LANES))
            ai = jnp.broadcast_to(aji_ref[k:k + 1, q * LANES:(q + 1) * LANES], (bsz, LANES))
            xr = carry_ref[:, lr:lr + LANES]
            xi = carry_ref[:, li:li + LANES]
            for c in range(n_c):
                rows = slice(c * bsz, (c + 1) * bsz)
                sr = st_ref[rows, lr:lr + LANES]
                si = st_ref[rows, li:li + LANES]
                st_ref[rows, lr:lr + LANES] = xr
                st_ref[rows, li:li + LANES] = xi
                xr, xi = ar * xr - ai * xi + sr, ar * xi + ai * xr + si
            carry_ref[:, lr:lr + LANES] = xr
            carry_ref[:, li:li + LANES] = xi

        xs = st_ref[:, k * k_states:(k + 1) * k_states].astype(BF16)
        y_ref[k] = _gelu_tanh(y_ref[k] + jnp.dot(xs, wout_ref[k], preferred_element_type=F32))

    for r in range(J):
        z = jnp.concatenate([y_ref[k, :, r * LANES:(r + 1) * LANES] for k in range(n_k)], axis=1)
        gate = jnp.dot(z.astype(BF16), wglu_ref[...], preferred_element_type=F32) + bglu_ref[...]
        o_r = z * (1.0 / (1.0 + jnp.exp(-gate)))
        for c in range(n_c):
            for k in range(n_k):
                oscr_ref[k, pl.ds(c * J + r, bsz, stride=P), :] = o_r[c * bsz:(c + 1) * bsz, k * LANES:(k + 1) * LANES]
    for bb in range(bsz):
        for k in range(n_k):
            o_ref[bb, :, k * LANES:(k + 1) * LANES] = oscr_ref[k, bb * P:bb * P + L, :].astype(o_ref.dtype)


def _s5_call(x, wu, win, wintra, wout, ajr, aji, wglu, bglu):
    bsz, seq, _ = x.shape
    L, J = S5_BLOCK, S5_CHUNK
    rows_c = (L // J) * bsz
    n_k = S5_WIDTH // LANES
    const = lambda shape: pl.BlockSpec(shape, lambda t: (0,) * len(shape))
    return pl.pallas_call(
        _s5_kernel,
        grid=(seq // L,),
        in_specs=[pl.BlockSpec((bsz, L, D_MODEL), lambda t: (0, t, 0))]
        + [const(a.shape) for a in (wu, win, wintra, wout, ajr, aji, wglu, bglu)],
        out_specs=pl.BlockSpec((bsz, L, S5_WIDTH), lambda t: (0, t, 0)),
        out_shape=jax.ShapeDtypeStruct((bsz, seq, S5_WIDTH), BF16),
        scratch_shapes=[
            pltpu.VMEM((n_k, bsz * S5_PITCH, LANES), F32),
            pltpu.VMEM((rows_c, 2 * S5_STATES), F32),
            pltpu.VMEM((bsz, 2 * S5_STATES), F32),
            pltpu.VMEM((n_k, rows_c, J * LANES), F32),
            pltpu.VMEM((n_k, bsz * S5_PITCH, LANES), F32),
        ],
        compiler_params=pltpu.CompilerParams(
            dimension_semantics=("arbitrary",), vmem_limit_bytes=VMEM_LIMIT_BYTES),
        name="s5_mixer",
    )(x, wu, win, wintra, wout, ajr, aji, wglu, bglu)


def _s5_prepare(a_re, a_im, log_dt, b_re, b_im, c_re, c_im, d_skip):
    J = S5_CHUNK
    n_k = S5_WIDTH // LANES
    gpk = S5_GROUPS // n_k
    lanes = gpk * S5_STATE
    hi = lax.Precision.HIGHEST

    per_state = lambda t: t.reshape(n_k, 1, lanes)
    ar, ai = per_state(a_re), per_state(a_im)
    dt = per_state(jnp.broadcast_to(jnp.exp(log_dt)[:, None], a_re.shape))
    b_t = lambda t: t.reshape(n_k, gpk, S5_STATE, S5_GROUP).transpose(0, 3, 1, 2).reshape(n_k, S5_GROUP, lanes)
    c_t = lambda t: t.reshape(n_k, gpk, S5_GROUP, S5_STATE).transpose(0, 2, 1, 3).reshape(n_k, S5_GROUP, lanes)
    b_re, b_im, c_re, c_im = b_t(b_re), b_t(b_im), c_t(c_re), c_t(c_im)

    def apow(m):
        mag = jnp.exp(m * ar * dt)
        return mag * jnp.cos(m * ai * dt), mag * jnp.sin(m * ai * dt)

    abr, abi = apow(1)
    den = ar * ar + ai * ai
    nr = abr - 1.0
    fr = (nr * ar + abi * ai) / den
    fi = (abi * ar - nr * ai) / den
    bbr = fr * b_re - fi * b_im
    bbi = fr * b_im + fi * b_re

    ab_re, ab_im = [], []
    for m in range(J):
        pr, pi = apow(m)
        ab_re.append(pr * bbr - pi * bbi)
        ab_im.append(pr * bbi + pi * bbr)

    def block_diag(t, lane_period, lane_group_width):
        n_lanes = t.shape[-1]
        row_g = lax.broadcasted_iota(jnp.int32, (gpk, 1, n_lanes), 0)
        lane_g = (lax.broadcasted_iota(jnp.int32, (gpk, 1, n_lanes), 2) % lane_period) // lane_group_width
        full = jnp.where(row_g == lane_g, t[:, :, None, :, :], 0.0)
        return full.reshape(n_k, J * LANES, n_lanes)

    win = block_diag(jnp.stack(
        [jnp.concatenate([ab_re[J - 1 - i], ab_im[J - 1 - i]], axis=-1) for i in range(J)], axis=1),
        lanes, S5_STATE)

    t_out = []
    for r in range(J):
        pr, pi = apow(r + 1)
        t_out.append(jnp.concatenate([c_re * pr - c_im * pi, -(c_re * pi + c_im * pr)], axis=-1))
    wout = block_diag(jnp.stack(t_out, axis=1), lanes, S5_STATE).transpose(0, 2, 1)

    grp = lambda t: t.reshape(n_k, S5_GROUP, gpk, S5_STATE)
    kmat = [(jnp.einsum('kqgn,kpgn->kpgq', grp(c_re), grp(ab_re[m]), precision=hi)
             - jnp.einsum('kqgn,kpgn->kpgq', grp(c_im), grp(ab_im[m]), precision=hi)).reshape(n_k, S5_GROUP, LANES)
            for m in range(J)]
    d_t = d_skip.reshape(n_k, gpk, S5_GROUP).transpose(0, 2, 1)
    p_eq_q = (lax.broadcasted_iota(jnp.int32, (S5_GROUP, gpk, S5_GROUP), 0)
              == lax.broadcasted_iota(jnp.int32, (S5_GROUP, gpk, S5_GROUP), 2))
    kmat[0] = kmat[0] + jnp.where(p_eq_q, d_t[..., None], 0.0).reshape(n_k, S5_GROUP, LANES)
    zero = jnp.zeros_like(kmat[0])
    t_intra = jnp.stack([jnp.concatenate([kmat[r - i] if i <= r else zero for r in range(J)], axis=-1)
                         for i in range(J)], axis=1)
    wintra = block_diag(t_intra, LANES, S5_GROUP)

    ajr, aji = apow(J)
    return (win.astype(BF16), wintra.astype(BF16), wout.astype(BF16), ajr.reshape(n_k, lanes), aji.reshape(n_k, lanes))


def _out_mlp_kernel(x_ref, og0_ref, og1_ref, os_ref, wo_ref, wup_ref, wdn_ref, fw_ref, o_ref):
    og = jnp.concatenate([og0_ref[...], og1_ref[...]], axis=0)
    x1 = x_ref[...] + jnp.dot(og, wo_ref[0:GLA_WIDTH, :], preferred_element_type=F32) \
        + jnp.dot(os_ref[...], wo_ref[GLA_WIDTH:, :], preferred_element_type=F32)
    h = _rms_scale(x1).astype(BF16)
    acts = []
    for c in range(D_FF // MLP_FF_CHUNK):
        cs = slice(c * MLP_FF_CHUNK, (c + 1) * MLP_FF_CHUNK)
        a = jnp.maximum(jnp.dot(h, wup_ref[:, cs], preferred_element_type=F32), 0.0)
        acts.append((a * a).astype(BF16))
    x2 = x1 + jnp.dot(jnp.concatenate(acts, axis=1), wdn_ref[...], preferred_element_type=F32)
    o_ref[...] = _rmsnorm(x2, fw_ref[...])


def _out_mlp_call(x2, og_blocks, os_, wo, wup, wdn, fw, seq):
    rows = x2.shape[0]
    nb, L = og_blocks.shape[1], og_blocks.shape[2]
    assert MLP_ROWS == 2 * L and seq % MLP_ROWS == 0
    tiles_per_seq = seq // MLP_ROWS
    const = lambda shape: pl.BlockSpec(shape, lambda i: (0,) * len(shape))
    tile = lambda width: pl.BlockSpec((MLP_ROWS, width), lambda i: (i, 0))

    def og_spec(e):
        def index(i):
            b, t = i // tiles_per_seq, i % tiles_per_seq
            return ((b // nb) * (seq // L) + 2 * t + e + 1, b % nb, 0, 0)
        return pl.BlockSpec((None, None, L, GLA_WIDTH), index)

    return pl.pallas_call(
        _out_mlp_kernel,
        grid=(rows // MLP_ROWS,),
        in_specs=[tile(D_MODEL), og_spec(0), og_spec(1), tile(S5_WIDTH)]
        + [const(a.shape) for a in (wo, wup, wdn, fw)],
        out_specs=tile(D_MODEL),
        out_shape=jax.ShapeDtypeStruct((rows, D_MODEL), F32),
        compiler_params=pltpu.CompilerParams(
            dimension_semantics=("arbitrary",), vmem_limit_bytes=VMEM_LIMIT_BYTES),
        name="out_mlp",
    )(x2, og_blocks, og_blocks, os_, wo, wup, wdn, fw)


def kernel(x, norm_mix_w, w_in, w_gk_up, b_gk, gla_norm_w, s5_a_re, s5_a_im, s5_log_dt, s5_b_re, s5_b_im,
           s5_c_re, s5_c_im, s5_d, w_glu, b_glu, w_out, norm_mlp_w, w_mlp_up, w_mlp_down, norm_final_w):
    bsz, seq, _ = x.shape
    assert bsz % GLA_SEQS == 0 and seq % GLA_BLOCK == 0 and seq % S5_BLOCK == 0 and (bsz * seq) % MLP_ROWS == 0
    assert norm_mix_w.shape[0] == 1, "single-layer problem"
    for l in range(1):
        w = w_in[l] * norm_mix_w[l][:, None]
        kd2, r = 2 * GLA_KEY_DIM, GLA_GATE_RANK
        wg = jnp.concatenate([
            w[:, 0:kd2 + GLA_WIDTH], w[:, kd2 + GLA_WIDTH + r:kd2 + 2 * GLA_WIDTH + r],
            w[:, kd2 + GLA_WIDTH:kd2 + GLA_WIDTH + r], jnp.zeros((D_MODEL, LANES - r), F32)], axis=1).astype(BF16)
        wu = w[:, kd2 + 2 * GLA_WIDTH + r:].astype(BF16)
        wup = jnp.concatenate([w_gk_up[l], jnp.zeros((LANES - r, GLA_KEY_DIM), F32)], axis=0)

        o_gla = _gla_call(x, wg, wup, b_gk[l][None, :], gla_norm_w[l][None, :])

        win, wintra, wout, ajr, aji = _s5_prepare(s5_a_re[l], s5_a_im[l], s5_log_dt[l], s5_b_re[l], s5_b_im[l],
                                                  s5_c_re[l], s5_c_im[l], s5_d[l])
        o_s5 = _s5_call(x, wu, win, wintra, wout, ajr, aji, w_glu[l].astype(BF16), b_glu[l][None, :])

        rows = bsz * seq
        out = _out_mlp_call(x.reshape(rows, D_MODEL), o_gla, o_s5.reshape(rows, S5_WIDTH),
                            w_out[l].astype(BF16), (w_mlp_up[l] * norm_mlp_w[l][:, None]).astype(BF16),
                            w_mlp_down[l].astype(BF16), norm_final_w[None, :], seq)
        x = out.reshape(bsz, seq, D_MODEL)
    return x
```

```python
import functools
import math

import jax
import jax.numpy as jnp
import numpy as np
from jax import lax
from jax.experimental import pallas as pl
from jax.experimental.pallas import tpu as pltpu

F32 = jnp.float32
BF16 = jnp.bfloat16

D_MODEL = 1024
GLA_WIDTH = 512
S5_WIDTH = 512
GLA_HEADS = 4
GLA_KEY_DIM = 256
GLA_DK = 64
GLA_DV = 128
GLA_GATE_RANK = 16
GLA_GATE_NORMALIZER = 16.0
GLA_CHUNK = 64
GLA_SPAN = 2 * GLA_CHUNK
S5_GROUP = 16
S5_GROUPS = 32
S5_STATE = 64
D_FF = 4096
EPS = 1e-6

LANES = 128
MXU_DIM = 256
S5_STATES = S5_GROUPS * S5_STATE
VMEM_LIMIT_BYTES = 56 * 1024 * 1024

GLA_BLOCK = 256
GLA_SEQS = 2
S5_BLOCK = 64
S5_CHUNK = 4
S5_PITCH = S5_BLOCK + 8
MLP_ROWS = 1024
MLP_FF_CHUNK = 1024
MLP_DOWN_ROWS = 256


def _rms_scale(x):
    return x * lax.rsqrt(jnp.mean(x * x, axis=-1, keepdims=True) + EPS)


def _rmsnorm(x, w):
    return _rms_scale(x) * w


def _bf16_parts(x, n):
    parts = []
    for _ in range(n):
        part = x.astype(BF16)
        parts.append(part)
        x = x - part.astype(F32)
    return parts


def _gla_substep(x_ref, p_new_ref, p_ref, st_ref, o_ref, first, wg_ref, w_cat, w_lo, bgk_ref, gnw_ref,
                 causal, tri, same_head, lane_head, span_of_t):
    nb, L, _ = x_ref.shape
    n_spans = L // GLA_SPAN
    seqs = range(nb)
    cols = lambda i, c0, c1: p_ref[i * L:(i + 1) * L, c0:c1]

    hb = _rms_scale(x_ref[...].reshape(nb * L, D_MODEL)).astype(BF16)
    n_cols = wg_ref.shape[1]
    tiles = [(c0, min(c0 + MXU_DIM, n_cols)) for c0 in range(0, n_cols, MXU_DIM)]

    def project_tile():
        if tiles:
            c0, c1 = tiles.pop(0)
            p_new_ref[:, c0:c1] = jnp.dot(hb, wg_ref[:, c0:c1], preferred_element_type=F32)

    log_a = []
    for i in seqs:
        gl_hi, gl_lo = _bf16_parts(cols(i, 1536, 1664), 2)
        gk = (jnp.dot(jnp.concatenate([gl_hi, gl_lo], axis=1), w_cat, preferred_element_type=F32)
              + jnp.dot(gl_hi, w_lo, preferred_element_type=F32) + bgk_ref[...])
        log_a.append((jnp.minimum(gk, 0.0) - jnp.log1p(jnp.exp(-jnp.abs(gk)))) * (1.0 / GLA_GATE_NORMALIZER))
    project_tile()

    b = []
    for i in seqs:
        terms = jnp.dot(tri, jnp.concatenate(_bf16_parts(log_a[i], 3), axis=1), preferred_element_type=F32)
        b.append(sum(terms[:, j * GLA_KEY_DIM:(j + 1) * GLA_KEY_DIM] for j in range(3)))
    span_row = lambda t, r: jnp.concatenate(
        [jnp.broadcast_to(t[m * GLA_SPAN + r:m * GLA_SPAN + r + 1, :], (GLA_SPAN, GLA_KEY_DIM))
         for m in range(n_spans)], axis=0)
    b_mid = [span_row(b[i], GLA_CHUNK - 1) for i in seqs]
    b_last = [span_row(b[i], GLA_SPAN - 1) for i in seqs]
    project_tile()

    q = [cols(i, 0, 256) * (GLA_DK ** -0.5) for i in seqs]
    q_s = [q[i] * jnp.exp(b[i] - b_mid[i]) for i in seqs]
    q_b = [(q[i] * jnp.exp(b[i])).astype(BF16) for i in seqs]
    k_et = [(cols(i, 256, 512) * jnp.exp(b_mid[i] - b[i])).T.astype(BF16) for i in seqs]
    k_dt = [(cols(i, 256, 512) * jnp.exp(b_last[i] - b[i])).T.astype(BF16) for i in seqs]
    b_t = [b[i].T for i in seqs]
    v_b = [cols(i, 512, 1024).astype(BF16) for i in seqs]
    project_tile()

    scores = []
    for i in seqs:
        q_heads = jnp.concatenate([jnp.where(lane_head == hh, q_s[i], 0.0) for hh in range(GLA_HEADS)], axis=0)
        scores.append(jnp.dot(q_heads.astype(BF16), k_et[i], preferred_element_type=F32))
    project_tile()
    o_intra = [[] for _ in seqs]
    for hh in range(GLA_HEADS):
        for i in seqs:
            s = jnp.where(causal, scores[i][hh * L:(hh + 1) * L], 0.0).astype(BF16)
            o_intra[i].append(jnp.dot(s, v_b[i][:, hh * GLA_DV:(hh + 1) * GLA_DV], preferred_element_type=F32))
        project_tile()

    st = [jnp.where(first, 0.0, st_ref[i]) for i in seqs]
    o_inter = [[] for _ in seqs]
    for m in range(n_spans):
        r0, r1 = m * GLA_SPAN, (m + 1) * GLA_SPAN
        for i in seqs:
            o_inter[i].append(jnp.dot(q_b[i][r0:r1], st[i].astype(BF16), preferred_element_type=F32))
            k_m = jnp.where(span_of_t == m, k_dt[i], jnp.zeros_like(k_dt[i]))
            upd = jnp.dot(k_m, v_b[i], preferred_element_type=F32)
            decay = jnp.exp(b_t[i][:, r1 - 1:r1])
            st[i] = jnp.where(same_head, st[i] * decay + upd, 0.0)
        project_tile()
        project_tile()

    for i in seqs:
        st_ref[i] = st[i]
        o = jnp.concatenate(o_intra[i], axis=1) + jnp.concatenate(o_inter[i], axis=0)
        outs = []
        for hh in range(GLA_HEADS):
            oh = o[:, hh * GLA_DV:(hh + 1) * GLA_DV]
            outs.append(oh * lax.rsqrt(jnp.mean(oh * oh, axis=-1, keepdims=True) + EPS) * gnw_ref[...])
        g = cols(i, 1024, 1536)
        o_ref[i] = (jnp.concatenate(outs, axis=1) * (g * (1.0 / (1.0 + jnp.exp(-g))))).astype(o_ref.dtype)
    while tiles:
        project_tile()


def _gla_kernel(blocks_per_seq, xa_ref, xb_ref, wg_ref, wup_ref, bgk_ref, gnw_ref, o_ref,
                st_ref, pa_ref, pb_ref):
    s = pl.program_id(0)
    L = xa_ref.shape[1]

    @pl.when(s == 0)
    def _():
        st_ref[...] = jnp.zeros_like(st_ref)
        pb_ref[...] = jnp.zeros_like(pb_ref)

    w_hi, w_lo = _bf16_parts(wup_ref[...], 2)
    w_cat = jnp.concatenate([w_hi, w_hi], axis=0)
    row = lax.broadcasted_iota(jnp.int32, (L, L), 0)
    col = lax.broadcasted_iota(jnp.int32, (L, L), 1)
    causal = ((row // GLA_SPAN) == (col // GLA_SPAN)) & (col <= row)
    tri = jnp.where(causal, 1.0, 0.0).astype(BF16)
    srow = lax.broadcasted_iota(jnp.int32, st_ref.shape[1:], 0) // GLA_DK
    scol = lax.broadcasted_iota(jnp.int32, st_ref.shape[1:], 1) // GLA_DV
    same_head = srow == scol
    lane_head = lax.broadcasted_iota(jnp.int32, (L, GLA_KEY_DIM), 1) // GLA_DK
    span_of_t = lax.broadcasted_iota(jnp.int32, (GLA_KEY_DIM, L), 1) // GLA_SPAN
    shared = (wg_ref, w_cat, w_lo, bgk_ref, gnw_ref, causal, tri, same_head, lane_head, span_of_t)

    _gla_substep(xa_ref, pa_ref, pb_ref, st_ref, o_ref.at[0], (2 * s - 1) % blocks_per_seq == 0, *shared)
    _gla_substep(xb_ref, pb_ref, pa_ref, st_ref, o_ref.at[1], (2 * s) % blocks_per_seq == 0, *shared)


def _gla_call(x, wg, wup, bgk, gnw):
    bsz, seq, _ = x.shape
    nb, L = GLA_SEQS, GLA_BLOCK
    bps = seq // L
    n_blocks = (bsz // nb) * bps
    assert n_blocks % 2 == 0 and bps % 2 == 0
    const = lambda shape: pl.BlockSpec(shape, lambda s: (0,) * len(shape))

    def x_spec(e):
        def index(s):
            n = jnp.minimum(2 * s + e, n_blocks - 1)
            return (n // bps, n % bps, 0)
        return pl.BlockSpec((nb, L, D_MODEL), index)

    return pl.pallas_call(
        functools.partial(_gla_kernel, bps),
        grid=(n_blocks // 2 + 1,),
        in_specs=[x_spec(0), x_spec(1), const(wg.shape), const(wup.shape), const(bgk.shape), const(gnw.shape)],
        out_specs=pl.BlockSpec((2, nb, L, GLA_WIDTH), lambda s: (s, 0, 0, 0)),
        out_shape=jax.ShapeDtypeStruct((n_blocks + 2, nb, L, GLA_WIDTH), BF16),
        scratch_shapes=[pltpu.VMEM((nb, GLA_HEADS * GLA_DK, GLA_HEADS * GLA_DV), F32),
                        pltpu.VMEM((nb * L, wg.shape[1]), F32),
                        pltpu.VMEM((nb * L, wg.shape[1]), F32)],
        compiler_params=pltpu.CompilerParams(
            dimension_semantics=("arbitrary",), vmem_limit_bytes=VMEM_LIMIT_BYTES),
        name="gla_mixer",
    )(x, x, wg, wup, bgk, gnw)


def _gelu_tanh(y):
    return 0.5 * y * (1.0 + jnp.tanh(np.float32(math.sqrt(2.0 / math.pi)) * (y + 0.044715 * (y * y * y))))


def _s5_kernel(x_ref, wu_ref, win_ref, wintra_ref, wout_ref, ajr_ref, aji_ref, wglu_ref, bglu_ref,
               o_ref, u_ref, st_ref, carry_ref, y_ref, oscr_ref):
    bsz, L, _ = x_ref.shape
    J, P = S5_CHUNK, S5_PITCH
    n_c = L // J
    n_k = S5_WIDTH // LANES
    k_states = 2 * S5_STATES // n_k
    q_pairs = k_states // 2 // LANES

    @pl.when(pl.program_id(0) == 0)
    def _():
        carry_ref[...] = jnp.zeros_like(carry_ref)

    h = _rms_scale(x_ref[...].reshape(bsz * L, D_MODEL))
    u = jnp.dot(h.astype(BF16), wu_ref[...], preferred_element_type=F32)
    for bb in range(bsz):
        for k in range(n_k):
            u_ref[k, bb * P:bb * P + L, :] = u[bb * L:(bb + 1) * L, k * LANES:(k + 1) * LANES]

    for k in range(n_k):
        utb = jnp.concatenate(
            [jnp.concatenate([u_ref[k, pl.ds(c * J + i, bsz, stride=P), :] for i in range(J)], axis=1)
             for c in range(n_c)], axis=0).astype(BF16)
        st_ref[:, k * k_states:(k + 1) * k_states] = jnp.dot(utb, win_ref[k], preferred_element_type=F32)
        y_ref[k] = jnp.dot(utb, wintra_ref[k], preferred_element_type=F32)

        for q in range(q_pairs):
            lr = k * k_states + q * LANES
            li = lr + k_states // 2
            ar = jnp.broadcast_to(ajr_ref[k:k + 1, q * LANES:(q + 1) * LANES], (bsz, LANES))
            ai = jnp.broadcast_to(aji_ref[k:k + 1, q * LANES:(q + 1) * LANES], (bsz, LANES))
            xr = carry_ref[:, lr:lr + LANES]
            xi = carry_ref[:, li:li + LANES]
            for c in range(n_c):
                rows = slice(c * bsz, (c + 1) * bsz)
                sr = st_ref[rows, lr:lr + LANES]
                si = st_ref[rows, li:li + LANES]
                st_ref[rows, lr:lr + LANES] = xr
                st_ref[rows, li:li + LANES] = xi
                xr, xi = ar * xr - ai * xi + sr, ar * xi + ai * xr + si
            carry_ref[:, lr:lr + LANES] = xr
            carry_ref[:, li:li + LANES] = xi

        xs = st_ref[:, k * k_states:(k + 1) * k_states].astype(BF16)
        y_ref[k] = _gelu_tanh(y_ref[k] + jnp.dot(xs, wout_ref[k], preferred_element_type=F32))

    for r in range(J):
        z = jnp.concatenate([y_ref[k, :, r * LANES:(r + 1) * LANES] for k in range(n_k)], axis=1)
        gate = jnp.dot(z.astype(BF16), wglu_ref[...], preferred_element_type=F32) + bglu_ref[...]
        o_r = z * (1.0 / (1.0 + jnp.exp(-gate)))
        for c in range(n_c):
            for k in range(n_k):
                oscr_ref[k, pl.ds(c * J + r, bsz, stride=P), :] = o_r[c * bsz:(c + 1) * bsz, k * LANES:(k + 1) * LANES]
    for bb in range(bsz):
        for k in range(n_k):
            o_ref[bb, :, k * LANES:(k + 1) * LANES] = oscr_ref[k, bb * P:bb * P + L, :].astype(o_ref.dtype)


def _s5_call(x, wu, win, wintra, wout, ajr, aji, wglu, bglu):
    bsz, seq, _ = x.shape
    L, J = S5_BLOCK, S5_CHUNK
    rows_c = (L // J) * bsz
    n_k = S5_WIDTH // LANES
    const = lambda shape: pl.BlockSpec(shape, lambda t: (0,) * len(shape))
    return pl.pallas_call(
        _s5_kernel,
        grid=(seq // L,),
        in_specs=[pl.BlockSpec((bsz, L, D_MODEL), lambda t: (0, t, 0))]
        + [const(a.shape) for a in (wu, win, wintra, wout, ajr, aji, wglu, bglu)],
        out_specs=pl.BlockSpec((bsz, L, S5_WIDTH), lambda t: (0, t, 0)),
        out_shape=jax.ShapeDtypeStruct((bsz, seq, S5_WIDTH), BF16),
        scratch_shapes=[
            pltpu.VMEM((n_k, bsz * S5_PITCH, LANES), F32),
            pltpu.VMEM((rows_c, 2 * S5_STATES), F32),
            pltpu.VMEM((bsz, 2 * S5_STATES), F32),
            pltpu.VMEM((n_k, rows_c, J * LANES), F32),
            pltpu.VMEM((n_k, bsz * S5_PITCH, LANES), F32),
        ],
        compiler_params=pltpu.CompilerParams(
            dimension_semantics=("arbitrary",), vmem_limit_bytes=VMEM_LIMIT_BYTES),
        name="s5_mixer",
    )(x, wu, win, wintra, wout, ajr, aji, wglu, bglu)


def _s5_prepare(a_re, a_im, log_dt, b_re, b_im, c_re, c_im, d_skip):
    J = S5_CHUNK
    n_k = S5_WIDTH // LANES
    gpk = S5_GROUPS // n_k
    lanes = gpk * S5_STATE
    hi = lax.Precision.HIGHEST

    per_state = lambda t: t.reshape(n_k, 1, lanes)
    ar, ai = per_state(a_re), per_state(a_im)
    dt = per_state(jnp.broadcast_to(jnp.exp(log_dt)[:, None], a_re.shape))
    b_t = lambda t: t.reshape(n_k, gpk, S5_STATE, S5_GROUP).transpose(0, 3, 1, 2).reshape(n_k, S5_GROUP, lanes)
    c_t = lambda t: t.reshape(n_k, gpk, S5_GROUP, S5_STATE).transpose(0, 2, 1, 3).reshape(n_k, S5_GROUP, lanes)
    b_re, b_im, c_re, c_im = b_t(b_re), b_t(b_im), c_t(c_re), c_t(c_im)

    def apow(m):
        mag = jnp.exp(m * ar * dt)
        return mag * jnp.cos(m * ai * dt), mag * jnp.sin(m * ai * dt)

    abr, abi = apow(1)
    den = ar * ar + ai * ai
    nr = abr - 1.0
    fr = (nr * ar + abi * ai) / den
    fi = (abi * ar - nr * ai) / den
    bbr = fr * b_re - fi * b_im
    bbi = fr * b_im + fi * b_re

    ab_re, ab_im = [], []
    for m in range(J):
        pr, pi = apow(m)
        ab_re.append(pr * bbr - pi * bbi)
        ab_im.append(pr * bbi + pi * bbr)

    def block_diag(t, lane_period, lane_group_width):
        n_lanes = t.shape[-1]
        row_g = lax.broadcasted_iota(jnp.int32, (gpk, 1, n_lanes), 0)
        lane_g = (lax.broadcasted_iota(jnp.int32, (gpk, 1, n_lanes), 2) % lane_period) // lane_group_width
        full = jnp.where(row_g == lane_g, t[:, :, None, :, :], 0.0)
        return full.reshape(n_k, J * LANES, n_lanes)

    win = block_diag(jnp.stack(
        [jnp.concatenate([ab_re[J - 1 - i], ab_im[J - 1 - i]], axis=-1) for i in range(J)], axis=1),
        lanes, S5_STATE)

    t_out = []
    for r in range(J):
        pr, pi = apow(r + 1)
        t_out.append(jnp.concatenate([c_re * pr - c_im * pi, -(c_re * pi + c_im * pr)], axis=-1))
    wout = block_diag(jnp.stack(t_out, axis=1), lanes, S5_STATE).transpose(0, 2, 1)

    grp = lambda t: t.reshape(n_k, S5_GROUP, gpk, S5_STATE)
    kmat = [(jnp.einsum('kqgn,kpgn->kpgq', grp(c_re), grp(ab_re[m]), precision=hi)
             - jnp.einsum('kqgn,kpgn->kpgq', grp(c_im), grp(ab_im[m]), precision=hi)).reshape(n_k, S5_GROUP, LANES)
            for m in range(J)]
    d_t = d_skip.reshape(n_k, gpk, S5_GROUP).transpose(0, 2, 1)
    p_eq_q = (lax.broadcasted_iota(jnp.int32, (S5_GROUP, gpk, S5_GROUP), 0)
              == lax.broadcasted_iota(jnp.int32, (S5_GROUP, gpk, S5_GROUP), 2))
    kmat[0] = kmat[0] + jnp.where(p_eq_q, d_t[..., None], 0.0).reshape(n_k, S5_GROUP, LANES)
    zero = jnp.zeros_like(kmat[0])
    t_intra = jnp.stack([jnp.concatenate([kmat[r - i] if i <= r else zero for r in range(J)], axis=-1)
                         for i in range(J)], axis=1)
    wintra = block_diag(t_intra, LANES, S5_GROUP)

    ajr, aji = apow(J)
    return (win.astype(BF16), wintra.astype(BF16), wout.astype(BF16), ajr.reshape(n_k, lanes), aji.reshape(n_k, lanes))


def _out_mlp_kernel(n_og, x_ref, *refs):
    og_refs = refs[:n_og]
    os_ref, wo_ref, wup_ref, wdn_ref, fw_ref, o_ref = refs[n_og:]
    og = jnp.concatenate([og_ref[...] for og_ref in og_refs], axis=0)
    x1 = x_ref[...] + jnp.dot(og, wo_ref[0:GLA_WIDTH, :], preferred_element_type=F32) \
        + jnp.dot(os_ref[...], wo_ref[GLA_WIDTH:, :], preferred_element_type=F32)
    h = _rms_scale(x1).astype(BF16)
    acts = []
    for c in range(D_FF // MLP_FF_CHUNK):
        cs = slice(c * MLP_FF_CHUNK, (c + 1) * MLP_FF_CHUNK)
        a = jnp.maximum(jnp.dot(h, wup_ref[:, cs], preferred_element_type=F32), 0.0)
        acts.append((a * a).astype(BF16))
    act = jnp.concatenate(acts, axis=1)
    for r0 in range(0, MLP_ROWS, MLP_DOWN_ROWS):
        r = slice(r0, r0 + MLP_DOWN_ROWS)
        x2 = x1[r] + jnp.dot(act[r], wdn_ref[...], preferred_element_type=F32)
        o_ref[r, :] = _rmsnorm(x2, fw_ref[...])


def _out_mlp_call(x2, og_blocks, os_, wo, wup, wdn, fw, seq):
    rows = x2.shape[0]
    nb, L = og_blocks.shape[1], og_blocks.shape[2]
    assert MLP_ROWS % L == 0 and seq % MLP_ROWS == 0
    n_og = MLP_ROWS // L
    tiles_per_seq = seq // MLP_ROWS
    const = lambda shape: pl.BlockSpec(shape, lambda i: (0,) * len(shape), pipeline_mode=pl.Buffered(1))
    tile = lambda width: pl.BlockSpec((MLP_ROWS, width), lambda i: (i, 0))

    def og_spec(e):
        def index(i):
            b, t = i // tiles_per_seq, i % tiles_per_seq
            return ((b // nb) * (seq // L) + n_og * t + e + 1, b % nb, 0, 0)
        return pl.BlockSpec((None, None, L, GLA_WIDTH), index)

    return pl.pallas_call(
        functools.partial(_out_mlp_kernel, n_og),
        grid=(rows // MLP_ROWS,),
        in_specs=[tile(D_MODEL)] + [og_spec(e) for e in range(n_og)] + [tile(S5_WIDTH)]
        + [const(a.shape) for a in (wo, wup, wdn, fw)],
        out_specs=tile(D_MODEL),
        out_shape=jax.ShapeDtypeStruct((rows, D_MODEL), F32),
        compiler_params=pltpu.CompilerParams(
            dimension_semantics=("arbitrary",), vmem_limit_bytes=VMEM_LIMIT_BYTES),
        name="out_mlp",
    )(x2, *([og_blocks] * n_og), os_, wo, wup, wdn, fw)


def kernel(x, norm_mix_w, w_in, w_gk_up, b_gk, gla_norm_w, s5_a_re, s5_a_im, s5_log_dt, s5_b_re, s5_b_im,
           s5_c_re, s5_c_im, s5_d, w_glu, b_glu, w_out, norm_mlp_w, w_mlp_up, w_mlp_down, norm_final_w):
    bsz, seq, _ = x.shape
    assert bsz % GLA_SEQS == 0 and seq % GLA_BLOCK == 0 and seq % S5_BLOCK == 0 and (bsz * seq) % MLP_ROWS == 0
    assert norm_mix_w.shape[0] == 1, "single-layer problem"
    for l in range(1):
        w = w_in[l] * norm_mix_w[l][:, None]
        kd2, r = 2 * GLA_KEY_DIM, GLA_GATE_RANK
        wg = jnp.concatenate([
            w[:, 0:kd2 + GLA_WIDTH], w[:, kd2 + GLA_WIDTH + r:kd2 + 2 * GLA_WIDTH + r],
            w[:, kd2 + GLA_WIDTH:kd2 + GLA_WIDTH + r], jnp.zeros((D_MODEL, LANES - r), F32)], axis=1).astype(BF16)
        wu = w[:, kd2 + 2 * GLA_WIDTH + r:].astype(BF16)
        wup = jnp.concatenate([w_gk_up[l], jnp.zeros((LANES - r, GLA_KEY_DIM), F32)], axis=0)

        o_gla = _gla_call(x, wg, wup, b_gk[l][None, :], gla_norm_w[l][None, :])

        win, wintra, wout, ajr, aji = _s5_prepare(s5_a_re[l], s5_a_im[l], s5_log_dt[l], s5_b_re[l], s5_b_im[l],
                                                  s5_c_re[l], s5_c_im[l], s5_d[l])
        o_s5 = _s5_call(x, wu, win, wintra, wout, ajr, aji, w_glu[l].astype(BF16), b_glu[l][None, :])

        rows = bsz * seq
        out = _out_mlp_call(x.reshape(rows, D_MODEL), o_gla, o_s5.reshape(rows, S5_WIDTH),
                            w_out[l].astype(BF16), (w_mlp_up[l] * norm_mlp_w[l][:, None]).astype(BF16),
                            w_mlp_down[l].astype(BF16), norm_final_w[None, :], seq)
        x = out.reshape(bsz, seq, D_MODEL)
    return x
```

```python
import functools
import math

import jax
import jax.numpy as jnp
import numpy as np
from jax import lax
from jax.experimental import pallas as pl
from jax.experimental.pallas import tpu as pltpu

F32 = jnp.float32
BF16 = jnp.bfloat16

D_MODEL = 1024
GLA_WIDTH = 512
S5_WIDTH = 512
GLA_HEADS = 4
GLA_KEY_DIM = 256
GLA_DK = 64
GLA_DV = 128
GLA_GATE_RANK = 16
GLA_GATE_NORMALIZER = 16.0
GLA_CHUNK = 64
GLA_SPAN = 2 * GLA_CHUNK
S5_GROUP = 16
S5_GROUPS = 32
S5_STATE = 64
D_FF = 4096
EPS = 1e-6

LANES = 128
MXU_DIM = 256
S5_STATES = S5_GROUPS * S5_STATE
VMEM_LIMIT_BYTES = 56 * 1024 * 1024

GLA_BLOCK = 256
GLA_SEQS = 2
S5_BLOCK = 64
S5_CHUNK = 4
S5_PITCH = S5_BLOCK + 8
MLP_ROWS = 1024
MLP_FF_CHUNK = 1024
MLP_DOWN_ROWS = 256


def _rms_scale(x):
    return x * lax.rsqrt(jnp.mean(x * x, axis=-1, keepdims=True) + EPS)


def _rmsnorm(x, w):
    return _rms_scale(x) * w


def _bf16_parts(x, n):
    parts = []
    for _ in range(n):
        part = x.astype(BF16)
        parts.append(part)
        x = x - part.astype(F32)
    return parts


def _gla_substep(x_ref, p_new_ref, p_ref, st_ref, o_ref, first, wg_ref, wgk_ref, bgk_ref, gnw_ref,
                 causal, tri, same_head, lane_head, span_of_t):
    nb, L, _ = x_ref.shape
    n_spans = L // GLA_SPAN
    seqs = range(nb)
    cols = lambda i, c0, c1: p_ref[i * L:(i + 1) * L, c0:c1]

    hb = _rms_scale(x_ref[...].reshape(nb * L, D_MODEL)).astype(BF16)
    n_cols = wg_ref.shape[1]
    tiles = [(wg_ref, c0, c0) for c0 in range(0, n_cols, MXU_DIM)] + [(wgk_ref, 0, n_cols)]

    def project_tile():
        if tiles:
            w_ref, c0, dst = tiles.pop(0)
            p_new_ref[:, dst:dst + MXU_DIM] = jnp.dot(hb, w_ref[:, c0:c0 + MXU_DIM], preferred_element_type=F32)

    log_a = []
    for i in seqs:
        gk = cols(i, n_cols, n_cols + GLA_KEY_DIM) + bgk_ref[...]
        log_a.append((jnp.minimum(gk, 0.0) - jnp.log1p(jnp.exp(-jnp.abs(gk)))) * (1.0 / GLA_GATE_NORMALIZER))
    project_tile()

    b = []
    for i in seqs:
        terms = jnp.dot(tri, jnp.concatenate(_bf16_parts(log_a[i], 3), axis=1), preferred_element_type=F32)
        b.append(sum(terms[:, j * GLA_KEY_DIM:(j + 1) * GLA_KEY_DIM] for j in range(3)))
    span_row = lambda t, r: jnp.concatenate(
        [jnp.broadcast_to(t[m * GLA_SPAN + r:m * GLA_SPAN + r + 1, :], (GLA_SPAN, GLA_KEY_DIM))
         for m in range(n_spans)], axis=0)
    b_mid = [span_row(b[i], GLA_CHUNK - 1) for i in seqs]
    b_last = [span_row(b[i], GLA_SPAN - 1) for i in seqs]
    project_tile()

    q = [cols(i, 0, 256) * (GLA_DK ** -0.5) for i in seqs]
    q_s = [q[i] * jnp.exp(b[i] - b_mid[i]) for i in seqs]
    q_b = [(q[i] * jnp.exp(b[i])).astype(BF16) for i in seqs]
    k_et = [(cols(i, 256, 512) * jnp.exp(b_mid[i] - b[i])).T.astype(BF16) for i in seqs]
    k_dt = [(cols(i, 256, 512) * jnp.exp(b_last[i] - b[i])).T.astype(BF16) for i in seqs]
    b_t = [b[i].T for i in seqs]
    v_b = [cols(i, 512, 1024).astype(BF16) for i in seqs]
    project_tile()

    scores = []
    for i in seqs:
        q_heads = jnp.concatenate([jnp.where(lane_head == hh, q_s[i], 0.0) for hh in range(GLA_HEADS)], axis=0)
        scores.append(jnp.dot(q_heads.astype(BF16), k_et[i], preferred_element_type=F32))
    project_tile()
    o_intra = [[] for _ in seqs]
    for hh in range(GLA_HEADS):
        for i in seqs:
            s = jnp.where(causal, scores[i][hh * L:(hh + 1) * L], 0.0).astype(BF16)
            o_intra[i].append(jnp.dot(s, v_b[i][:, hh * GLA_DV:(hh + 1) * GLA_DV], preferred_element_type=F32))
        project_tile()

    st = [jnp.where(first, 0.0, st_ref[i]) for i in seqs]
    o_inter = [[] for _ in seqs]
    for m in range(n_spans):
        r0, r1 = m * GLA_SPAN, (m + 1) * GLA_SPAN
        for i in seqs:
            o_inter[i].append(jnp.dot(q_b[i][r0:r1], st[i].astype(BF16), preferred_element_type=F32))
            k_m = jnp.where(span_of_t == m, k_dt[i], jnp.zeros_like(k_dt[i]))
            upd = jnp.dot(k_m, v_b[i], preferred_element_type=F32)
            decay = jnp.exp(b_t[i][:, r1 - 1:r1])
            st[i] = jnp.where(same_head, st[i] * decay + upd, 0.0)
        project_tile()
        project_tile()

    for i in seqs:
        st_ref[i] = st[i]
        o = jnp.concatenate(o_intra[i], axis=1) + jnp.concatenate(o_inter[i], axis=0)
        outs = []
        for hh in range(GLA_HEADS):
            oh = o[:, hh * GLA_DV:(hh + 1) * GLA_DV]
            outs.append(oh * lax.rsqrt(jnp.mean(oh * oh, axis=-1, keepdims=True) + EPS) * gnw_ref[...])
        g = cols(i, 1024, 1536)
        o_ref[i] = (jnp.concatenate(outs, axis=1) * (g * (1.0 / (1.0 + jnp.exp(-g))))).astype(o_ref.dtype)
    while tiles:
        project_tile()


def _gla_kernel(blocks_per_seq, xa_ref, xb_ref, wg_ref, wgl_ref, wup_ref, bgk_ref, gnw_ref, o_ref,
                st_ref, pa_ref, pb_ref, wgk_ref):
    s = pl.program_id(0)
    L = xa_ref.shape[1]

    @pl.when(s == 0)
    def _():
        st_ref[...] = jnp.zeros_like(st_ref)
        pb_ref[...] = jnp.zeros_like(pb_ref)
        wgk_ref[...] = jnp.dot(wgl_ref[...], wup_ref[...], precision=lax.Precision.HIGHEST,
                               preferred_element_type=F32).astype(BF16)

    row = lax.broadcasted_iota(jnp.int32, (L, L), 0)
    col = lax.broadcasted_iota(jnp.int32, (L, L), 1)
    causal = ((row // GLA_SPAN) == (col // GLA_SPAN)) & (col <= row)
    tri = jnp.where(causal, 1.0, 0.0).astype(BF16)
    srow = lax.broadcasted_iota(jnp.int32, st_ref.shape[1:], 0) // GLA_DK
    scol = lax.broadcasted_iota(jnp.int32, st_ref.shape[1:], 1) // GLA_DV
    same_head = srow == scol
    lane_head = lax.broadcasted_iota(jnp.int32, (L, GLA_KEY_DIM), 1) // GLA_DK
    span_of_t = lax.broadcasted_iota(jnp.int32, (GLA_KEY_DIM, L), 1) // GLA_SPAN
    shared = (wg_ref, wgk_ref, bgk_ref, gnw_ref, causal, tri, same_head, lane_head, span_of_t)

    _gla_substep(xa_ref, pa_ref, pb_ref, st_ref, o_ref.at[0], (2 * s - 1) % blocks_per_seq == 0, *shared)
    _gla_substep(xb_ref, pb_ref, pa_ref, st_ref, o_ref.at[1], (2 * s) % blocks_per_seq == 0, *shared)


def _gla_call(x, wg, wgl, wup, bgk, gnw):
    bsz, seq, _ = x.shape
    nb, L = GLA_SEQS, GLA_BLOCK
    bps = seq // L
    n_blocks = (bsz // nb) * bps
    assert n_blocks % 2 == 0 and bps % 2 == 0
    const = lambda shape: pl.BlockSpec(shape, lambda s: (0,) * len(shape))

    def x_spec(e):
        def index(s):
            n = jnp.minimum(2 * s + e, n_blocks - 1)
            return (n // bps, n % bps, 0)
        return pl.BlockSpec((nb, L, D_MODEL), index)

    return pl.pallas_call(
        functools.partial(_gla_kernel, bps),
        grid=(n_blocks // 2 + 1,),
        in_specs=[x_spec(0), x_spec(1), const(wg.shape), const(wgl.shape), const(wup.shape), const(bgk.shape),
                  const(gnw.shape)],
        out_specs=pl.BlockSpec((2, nb, L, GLA_WIDTH), lambda s: (s, 0, 0, 0)),
        out_shape=jax.ShapeDtypeStruct((n_blocks + 2, nb, L, GLA_WIDTH), BF16),
        scratch_shapes=[pltpu.VMEM((nb, GLA_HEADS * GLA_DK, GLA_HEADS * GLA_DV), F32),
                        pltpu.VMEM((nb * L, wg.shape[1] + GLA_KEY_DIM), F32),
                        pltpu.VMEM((nb * L, wg.shape[1] + GLA_KEY_DIM), F32),
                        pltpu.VMEM((D_MODEL, GLA_KEY_DIM), BF16)],
        compiler_params=pltpu.CompilerParams(
            dimension_semantics=("arbitrary",), vmem_limit_bytes=VMEM_LIMIT_BYTES),
        name="gla_mixer",
    )(x, x, wg, wgl, wup, bgk, gnw)


def _gelu_tanh(y):
    return 0.5 * y * (1.0 + jnp.tanh(np.float32(math.sqrt(2.0 / math.pi)) * (y + 0.044715 * (y * y * y))))


def _s5_kernel(x_ref, wu_ref, win_ref, wintra_ref, wout_ref, ajr_ref, aji_ref, wglu_ref, bglu_ref,
               o_ref, u_ref, st_ref, carry_ref, y_ref, oscr_ref):
    bsz, L, _ = x_ref.shape
    J, P = S5_CHUNK, S5_PITCH
    n_c = L // J
    n_k = S5_WIDTH // LANES
    k_states = 2 * S5_STATES // n_k
    q_pairs = k_states // 2 // LANES

    @pl.when(pl.program_id(0) == 0)
    def _():
        carry_ref[...] = jnp.zeros_like(carry_ref)

    h = _rms_scale(x_ref[...].reshape(bsz * L, D_MODEL))
    u = jnp.dot(h.astype(BF16), wu_ref[...], preferred_element_type=F32)
    for bb in range(bsz):
        for k in range(n_k):
            u_ref[k, bb * P:bb * P + L, :] = u[bb * L:(bb + 1) * L, k * LANES:(k + 1) * LANES]

    for k in range(n_k):
        utb = jnp.concatenate(
            [jnp.concatenate([u_ref[k, pl.ds(c * J + i, bsz, stride=P), :] for i in range(J)], axis=1)
             for c in range(n_c)], axis=0).astype(BF16)
        st_ref[:, k * k_states:(k + 1) * k_states] = jnp.dot(utb, win_ref[k], preferred_element_type=F32)
        y_ref[k] = jnp.dot(utb, wintra_ref[k], preferred_element_type=F32)

        for q in range(q_pairs):
            lr = k * k_states + q * LANES
            li = lr + k_states // 2
            ar = jnp.broadcast_to(ajr_ref[k:k + 1, q * LANES:(q + 1) * LANES], (bsz, LANES))
            ai = jnp.broadcast_to(aji_ref[k:k + 1, q * LANES:(q + 1) * LANES], (bsz, LANES))
            xr = carry_ref[:, lr:lr + LANES]
            xi = carry_ref[:, li:li + LANES]
            for c in range(n_c):
                rows = slice(c * bsz, (c + 1) * bsz)
                sr = st_ref[rows, lr:lr + LANES]
                si = st_ref[rows, li:li + LANES]
                st_ref[rows, lr:lr + LANES] = xr
                st_ref[rows, li:li + LANES] = xi
                xr, xi = ar * xr - ai * xi + sr, ar * xi + ai * xr + si
            carry_ref[:, lr:lr + LANES] = xr
            carry_ref[:, li:li + LANES] = xi

        xs = st_ref[:, k * k_states:(k + 1) * k_states].astype(BF16)
        y_ref[k] = _gelu_tanh(y_ref[k] + jnp.dot(xs, wout_ref[k], preferred_element_type=F32))

    for r in range(J):
        z = jnp.concatenate([y_ref[k, :, r * LANES:(r + 1) * LANES] for k in range(n_k)], axis=1)
        gate = jnp.dot(z.astype(BF16), wglu_ref[...], preferred_element_type=F32) + bglu_ref[...]
        o_r = z * (1.0 / (1.0 + jnp.exp(-gate)))
        for c in range(n_c):
            for k in range(n_k):
                oscr_ref[k, pl.ds(c * J + r, bsz, stride=P), :] = o_r[c * bsz:(c + 1) * bsz, k * LANES:(k + 1) * LANES]
    for bb in range(bsz):
        for k in range(n_k):
            o_ref[bb, :, k * LANES:(k + 1) * LANES] = oscr_ref[k, bb * P:bb * P + L, :].astype(o_ref.dtype)


def _s5_call(x, wu, win, wintra, wout, ajr, aji, wglu, bglu):
    bsz, seq, _ = x.shape
    L, J = S5_BLOCK, S5_CHUNK
    rows_c = (L // J) * bsz
    n_k = S5_WIDTH // LANES
    const = lambda shape: pl.BlockSpec(shape, lambda t: (0,) * len(shape))
    return pl.pallas_call(
        _s5_kernel,
        grid=(seq // L,),
        in_specs=[pl.BlockSpec((bsz, L, D_MODEL), lambda t: (0, t, 0))]
        + [const(a.shape) for a in (wu, win, wintra, wout, ajr, aji, wglu, bglu)],
        out_specs=pl.BlockSpec((bsz, L, S5_WIDTH), lambda t: (0, t, 0)),
        out_shape=jax.ShapeDtypeStruct((bsz, seq, S5_WIDTH), BF16),
        scratch_shapes=[
            pltpu.VMEM((n_k, bsz * S5_PITCH, LANES), F32),
            pltpu.VMEM((rows_c, 2 * S5_STATES), F32),
            pltpu.VMEM((bsz, 2 * S5_STATES), F32),
            pltpu.VMEM((n_k, rows_c, J * LANES), F32),
            pltpu.VMEM((n_k, bsz * S5_PITCH, LANES), F32),
        ],
        compiler_params=pltpu.CompilerParams(
            dimension_semantics=("arbitrary",), vmem_limit_bytes=VMEM_LIMIT_BYTES),
        name="s5_mixer",
    )(x, wu, win, wintra, wout, ajr, aji, wglu, bglu)


def _s5_prepare(a_re, a_im, log_dt, b_re, b_im, c_re, c_im, d_skip):
    J = S5_CHUNK
    n_k = S5_WIDTH // LANES
    gpk = S5_GROUPS // n_k
    lanes = gpk * S5_STATE
    hi = lax.Precision.HIGHEST

    per_state = lambda t: t.reshape(n_k, 1, lanes)
    ar, ai = per_state(a_re), per_state(a_im)
    dt = per_state(jnp.broadcast_to(jnp.exp(log_dt)[:, None], a_re.shape))
    b_t = lambda t: t.reshape(n_k, gpk, S5_STATE, S5_GROUP).transpose(0, 3, 1, 2).reshape(n_k, S5_GROUP, lanes)
    c_t = lambda t: t.reshape(n_k, gpk, S5_GROUP, S5_STATE).transpose(0, 2, 1, 3).reshape(n_k, S5_GROUP, lanes)
    b_re, b_im, c_re, c_im = b_t(b_re), b_t(b_im), c_t(c_re), c_t(c_im)

    def apow(m):
        mag = jnp.exp(m * ar * dt)
        return mag * jnp.cos(m * ai * dt), mag * jnp.sin(m * ai * dt)

    abr, abi = apow(1)
    den = ar * ar + ai * ai
    nr = abr - 1.0
    fr = (nr * ar + abi * ai) / den
    fi = (abi * ar - nr * ai) / den
    bbr = fr * b_re - fi * b_im
    bbi = fr * b_im + fi * b_re

    ab_re, ab_im = [], []
    for m in range(J):
        pr, pi = apow(m)
        ab_re.append(pr * bbr - pi * bbi)
        ab_im.append(pr * bbi + pi * bbr)

    def block_diag(t, lane_period, lane_group_width):
        n_lanes = t.shape[-1]
        row_g = lax.broadcasted_iota(jnp.int32, (gpk, 1, n_lanes), 0)
        lane_g = (lax.broadcasted_iota(jnp.int32, (gpk, 1, n_lanes), 2) % lane_period) // lane_group_width
        full = jnp.where(row_g == lane_g, t[:, :, None, :, :], 0.0)
        return full.reshape(n_k, J * LANES, n_lanes)

    win = block_diag(jnp.stack(
        [jnp.concatenate([ab_re[J - 1 - i], ab_im[J - 1 - i]], axis=-1) for i in range(J)], axis=1),
        lanes, S5_STATE)

    t_out = []
    for r in range(J):
        pr, pi = apow(r + 1)
        t_out.append(jnp.concatenate([c_re * pr - c_im * pi, -(c_re * pi + c_im * pr)], axis=-1))
    wout = block_diag(jnp.stack(t_out, axis=1), lanes, S5_STATE).transpose(0, 2, 1)

    grp = lambda t: t.reshape(n_k, S5_GROUP, gpk, S5_STATE)
    kmat = [(jnp.einsum('kqgn,kpgn->kpgq', grp(c_re), grp(ab_re[m]), precision=hi)
             - jnp.einsum('kqgn,kpgn->kpgq', grp(c_im), grp(ab_im[m]), precision=hi)).reshape(n_k, S5_GROUP, LANES)
            for m in range(J)]
    d_t = d_skip.reshape(n_k, gpk, S5_GROUP).transpose(0, 2, 1)
    p_eq_q = (lax.broadcasted_iota(jnp.int32, (S5_GROUP, gpk, S5_GROUP), 0)
              == lax.broadcasted_iota(jnp.int32, (S5_GROUP, gpk, S5_GROUP), 2))
    kmat[0] = kmat[0] + jnp.where(p_eq_q, d_t[..., None], 0.0).reshape(n_k, S5_GROUP, LANES)
    zero = jnp.zeros_like(kmat[0])
    t_intra = jnp.stack([jnp.concatenate([kmat[r - i] if i <= r else zero for r in range(J)], axis=-1)
                         for i in range(J)], axis=1)
    wintra = block_diag(t_intra, LANES, S5_GROUP)

    ajr, aji = apow(J)
    return (win.astype(BF16), wintra.astype(BF16), wout.astype(BF16), ajr.reshape(n_k, lanes), aji.reshape(n_k, lanes))


def _out_mlp_kernel(n_og, x_ref, *refs):
    og_refs = refs[:n_og]
    os_ref, wo_ref, wup_ref, wdn_ref, fw_ref, o_ref = refs[n_og:]
    og = jnp.concatenate([og_ref[...] for og_ref in og_refs], axis=0)
    x1 = x_ref[...] + jnp.dot(og, wo_ref[0:GLA_WIDTH, :], preferred_element_type=F32) \
        + jnp.dot(os_ref[...], wo_ref[GLA_WIDTH:, :], preferred_element_type=F32)
    h = _rms_scale(x1).astype(BF16)
    acts = []
    for c in range(D_FF // MLP_FF_CHUNK):
        cs = slice(c * MLP_FF_CHUNK, (c + 1) * MLP_FF_CHUNK)
        a = jnp.maximum(jnp.dot(h, wup_ref[:, cs], preferred_element_type=F32), 0.0)
        acts.append((a * a).astype(BF16))
    act = jnp.concatenate(acts, axis=1)
    for r0 in range(0, MLP_ROWS, MLP_DOWN_ROWS):
        r = slice(r0, r0 + MLP_DOWN_ROWS)
        x2 = x1[r] + jnp.dot(act[r], wdn_ref[...], preferred_element_type=F32)
        o_ref[r, :] = _rmsnorm(x2, fw_ref[...])


def _out_mlp_call(x2, og_blocks, os_, wo, wup, wdn, fw, seq):
    rows = x2.shape[0]
    nb, L = og_blocks.shape[1], og_blocks.shape[2]
    assert MLP_ROWS % L == 0 and seq % MLP_ROWS == 0
    n_og = MLP_ROWS // L
    tiles_per_seq = seq // MLP_ROWS
    const = lambda shape: pl.BlockSpec(shape, lambda i: (0,) * len(shape), pipeline_mode=pl.Buffered(1))
    tile = lambda width: pl.BlockSpec((MLP_ROWS, width), lambda i: (i, 0))

    def og_spec(e):
        def index(i):
            b, t = i // tiles_per_seq, i % tiles_per_seq
            return ((b // nb) * (seq // L) + n_og * t + e + 1, b % nb, 0, 0)
        return pl.BlockSpec((None, None, L, GLA_WIDTH), index)

    return pl.pallas_call(
        functools.partial(_out_mlp_kernel, n_og),
        grid=(rows // MLP_ROWS,),
        in_specs=[tile(D_MODEL)] + [og_spec(e) for e in range(n_og)] + [tile(S5_WIDTH)]
        + [const(a.shape) for a in (wo, wup, wdn, fw)],
        out_specs=tile(D_MODEL),
        out_shape=jax.ShapeDtypeStruct((rows, D_MODEL), F32),
        compiler_params=pltpu.CompilerParams(
            dimension_semantics=("arbitrary",), vmem_limit_bytes=VMEM_LIMIT_BYTES),
        name="out_mlp",
    )(x2, *([og_blocks] * n_og), os_, wo, wup, wdn, fw)


def kernel(x, norm_mix_w, w_in, w_gk_up, b_gk, gla_norm_w, s5_a_re, s5_a_im, s5_log_dt, s5_b_re, s5_b_im,
           s5_c_re, s5_c_im, s5_d, w_glu, b_glu, w_out, norm_mlp_w, w_mlp_up, w_mlp_down, norm_final_w):
    bsz, seq, _ = x.shape
    assert bsz % GLA_SEQS == 0 and seq % GLA_BLOCK == 0 and seq % S5_BLOCK == 0 and (bsz * seq) % MLP_ROWS == 0
    assert norm_mix_w.shape[0] == 1, "single-layer problem"
    for l in range(1):
        w = w_in[l] * norm_mix_w[l][:, None]
        kd2, r = 2 * GLA_KEY_DIM, GLA_GATE_RANK
        wg = jnp.concatenate([w[:, 0:kd2 + GLA_WIDTH], w[:, kd2 + GLA_WIDTH + r:kd2 + 2 * GLA_WIDTH + r]],
                             axis=1).astype(BF16)
        wgl = jnp.concatenate([w[:, kd2 + GLA_WIDTH:kd2 + GLA_WIDTH + r], jnp.zeros((D_MODEL, LANES - r), F32)], axis=1)
        wu = w[:, kd2 + 2 * GLA_WIDTH + r:].astype(BF16)
        wup = jnp.concatenate([w_gk_up[l], jnp.zeros((LANES - r, GLA_KEY_DIM), F32)], axis=0)

        o_gla = _gla_call(x, wg, wgl, wup, b_gk[l][None, :], gla_norm_w[l][None, :])

        win, wintra, wout, ajr, aji = _s5_prepare(s5_a_re[l], s5_a_im[l], s5_log_dt[l], s5_b_re[l], s5_b_im[l],
                                                  s5_c_re[l], s5_c_im[l], s5_d[l])
        o_s5 = _s5_call(x, wu, win, wintra, wout, ajr, aji, w_glu[l].astype(BF16), b_glu[l][None, :])

        rows = bsz * seq
        out = _out_mlp_call(x.reshape(rows, D_MODEL), o_gla, o_s5.reshape(rows, S5_WIDTH),
                            w_out[l].astype(BF16), (w_mlp_up[l] * norm_mlp_w[l][:, None]).astype(BF16),
                            w_mlp_down[l].astype(BF16), norm_final_w[None, :], seq)
        x = out.reshape(bsz, seq, D_MODEL)
    return x
```

```python
import functools
import math

import jax
import jax.numpy as jnp
import numpy as np
from jax import lax
from jax.experimental import pallas as pl
from jax.experimental.pallas import tpu as pltpu

F32 = jnp.float32
BF16 = jnp.bfloat16

D_MODEL = 1024
GLA_WIDTH = 512
S5_WIDTH = 512
GLA_HEADS = 4
GLA_KEY_DIM = 256
GLA_DK = 64
GLA_DV = 128
GLA_GATE_RANK = 16
GLA_GATE_NORMALIZER = 16.0
GLA_CHUNK = 64
GLA_SPAN = 2 * GLA_CHUNK
S5_GROUP = 16
S5_GROUPS = 32
S5_STATE = 64
D_FF = 4096
EPS = 1e-6

LANES = 128
MXU_DIM = 256
S5_STATES = S5_GROUPS * S5_STATE
VMEM_LIMIT_BYTES = 56 * 1024 * 1024

GLA_BLOCK = 256
GLA_SEQS = 2
S5_BLOCK = 64
S5_CHUNK = 4
S5_PITCH = S5_BLOCK + 8
MLP_ROWS = 1024
MLP_FF_CHUNK = 1024
MLP_DOWN_ROWS = 256


def _rms_scale(x):
    return x * lax.rsqrt(jnp.mean(x * x, axis=-1, keepdims=True) + EPS)


def _rmsnorm(x, w):
    return _rms_scale(x) * w


def _bf16_parts(x, n):
    parts = []
    for _ in range(n):
        part = x.astype(BF16)
        parts.append(part)
        x = x - part.astype(F32)
    return parts


def _gla_substep(x_ref, p_new_ref, p_ref, st_ref, o_ref, first, wg_ref, wgk_ref, bgk_ref, gnw_ref,
                 causal, tri, same_head, lane_head, span_of_t):
    nb, L, _ = x_ref.shape
    n_spans = L // GLA_SPAN
    seqs = range(nb)
    cols = lambda i, c0, c1: p_ref[i * L:(i + 1) * L, c0:c1]

    hb = _rms_scale(x_ref[...].reshape(nb * L, D_MODEL)).astype(BF16)
    n_cols = wg_ref.shape[1]
    tiles = [(wg_ref, c0, c0) for c0 in range(0, n_cols, MXU_DIM)] + [(wgk_ref, 0, n_cols)]

    def project_tile():
        if tiles:
            w_ref, c0, dst = tiles.pop(0)
            p_new_ref[:, dst:dst + MXU_DIM] = jnp.dot(hb, w_ref[:, c0:c0 + MXU_DIM], preferred_element_type=F32)

    log_a = []
    for i in seqs:
        gk = cols(i, n_cols, n_cols + GLA_KEY_DIM) + bgk_ref[...]
        log_a.append((jnp.minimum(gk, 0.0) - jnp.log1p(jnp.exp(-jnp.abs(gk)))) * (1.0 / GLA_GATE_NORMALIZER))
    project_tile()

    b = []
    for i in seqs:
        terms = jnp.dot(tri, jnp.concatenate(_bf16_parts(log_a[i], 3), axis=1), preferred_element_type=F32)
        b.append(sum(terms[:, j * GLA_KEY_DIM:(j + 1) * GLA_KEY_DIM] for j in range(3)))
    span_row = lambda t, r: jnp.concatenate(
        [jnp.broadcast_to(t[m * GLA_SPAN + r:m * GLA_SPAN + r + 1, :], (GLA_SPAN, GLA_KEY_DIM))
         for m in range(n_spans)], axis=0)
    b_mid = [span_row(b[i], GLA_CHUNK - 1) for i in seqs]
    b_last = [span_row(b[i], GLA_SPAN - 1) for i in seqs]
    project_tile()

    q = [cols(i, 0, 256) * (GLA_DK ** -0.5) for i in seqs]
    q_s = [q[i] * jnp.exp(b[i] - b_mid[i]) for i in seqs]
    q_b = [(q[i] * jnp.exp(b[i])).astype(BF16) for i in seqs]
    k_et = [(cols(i, 256, 512) * jnp.exp(b_mid[i] - b[i])).T.astype(BF16) for i in seqs]
    k_dt = [(cols(i, 256, 512) * jnp.exp(b_last[i] - b[i])).T.astype(BF16) for i in seqs]
    b_t = [b[i].T for i in seqs]
    v_b = [cols(i, 512, 1024).astype(BF16) for i in seqs]
    project_tile()

    scores = []
    for i in seqs:
        q_heads = jnp.concatenate([jnp.where(lane_head == hh, q_s[i], 0.0) for hh in range(GLA_HEADS)], axis=0)
        scores.append(jnp.dot(q_heads.astype(BF16), k_et[i], preferred_element_type=F32))
    project_tile()
    o_intra = [[] for _ in seqs]
    for hh in range(GLA_HEADS):
        for i in seqs:
            s = jnp.where(causal, scores[i][hh * L:(hh + 1) * L], 0.0).astype(BF16)
            o_intra[i].append(jnp.dot(s, v_b[i][:, hh * GLA_DV:(hh + 1) * GLA_DV], preferred_element_type=F32))
        project_tile()

    st = [jnp.where(first, 0.0, st_ref[i]) for i in seqs]
    o_inter = [[] for _ in seqs]
    for m in range(n_spans):
        r0, r1 = m * GLA_SPAN, (m + 1) * GLA_SPAN
        for i in seqs:
            o_inter[i].append(jnp.dot(q_b[i][r0:r1], st[i].astype(BF16), preferred_element_type=F32))
            k_m = jnp.where(span_of_t == m, k_dt[i], jnp.zeros_like(k_dt[i]))
            upd = jnp.dot(k_m, v_b[i], preferred_element_type=F32)
            decay = jnp.exp(b_t[i][:, r1 - 1:r1])
            st[i] = jnp.where(same_head, st[i] * decay + upd, 0.0)
        project_tile()
        project_tile()

    for i in seqs:
        st_ref[i] = st[i]
        o = jnp.concatenate(o_intra[i], axis=1) + jnp.concatenate(o_inter[i], axis=0)
        outs = []
        for hh in range(GLA_HEADS):
            oh = o[:, hh * GLA_DV:(hh + 1) * GLA_DV]
            outs.append(oh * lax.rsqrt(jnp.mean(oh * oh, axis=-1, keepdims=True) + EPS) * gnw_ref[...])
        g = cols(i, 1024, 1536)
        o_ref[i] = (jnp.concatenate(outs, axis=1) * (g * (1.0 / (1.0 + jnp.exp(-g))))).astype(o_ref.dtype)
    while tiles:
        project_tile()


def _gla_kernel(blocks_per_seq, xa_ref, xb_ref, wg_ref, wgl_ref, wup_ref, bgk_ref, gnw_ref, o_ref,
                st_ref, pa_ref, pb_ref, wgk_ref):
    s = pl.program_id(0)
    L = xa_ref.shape[1]

    @pl.when(s == 0)
    def _():
        st_ref[...] = jnp.zeros_like(st_ref)
        pb_ref[...] = jnp.zeros_like(pb_ref)
        wgk_ref[...] = jnp.dot(wgl_ref[...], wup_ref[...], precision=lax.Precision.HIGHEST,
                               preferred_element_type=F32).astype(BF16)

    row = lax.broadcasted_iota(jnp.int32, (L, L), 0)
    col = lax.broadcasted_iota(jnp.int32, (L, L), 1)
    causal = ((row // GLA_SPAN) == (col // GLA_SPAN)) & (col <= row)
    tri = jnp.where(causal, 1.0, 0.0).astype(BF16)
    srow = lax.broadcasted_iota(jnp.int32, st_ref.shape[1:], 0) // GLA_DK
    scol = lax.broadcasted_iota(jnp.int32, st_ref.shape[1:], 1) // GLA_DV
    same_head = srow == scol
    lane_head = lax.broadcasted_iota(jnp.int32, (L, GLA_KEY_DIM), 1) // GLA_DK
    span_of_t = lax.broadcasted_iota(jnp.int32, (GLA_KEY_DIM, L), 1) // GLA_SPAN
    shared = (wg_ref, wgk_ref, bgk_ref, gnw_ref, causal, tri, same_head, lane_head, span_of_t)

    _gla_substep(xa_ref, pa_ref, pb_ref, st_ref, o_ref.at[0], (2 * s - 1) % blocks_per_seq == 0, *shared)
    _gla_substep(xb_ref, pb_ref, pa_ref, st_ref, o_ref.at[1], (2 * s) % blocks_per_seq == 0, *shared)


def _gla_call(x, wg, wgl, wup, bgk, gnw):
    bsz, seq, _ = x.shape
    nb, L = GLA_SEQS, GLA_BLOCK
    bps = seq // L
    n_blocks = (bsz // nb) * bps
    assert n_blocks % 2 == 0 and bps % 2 == 0
    const = lambda shape: pl.BlockSpec(shape, lambda s: (0,) * len(shape))

    def x_spec(e):
        def index(s):
            n = jnp.minimum(2 * s + e, n_blocks - 1)
            return (n // bps, n % bps, 0)
        return pl.BlockSpec((nb, L, D_MODEL), index)

    return pl.pallas_call(
        functools.partial(_gla_kernel, bps),
        grid=(n_blocks // 2 + 1,),
        in_specs=[x_spec(0), x_spec(1), const(wg.shape), const(wgl.shape), const(wup.shape), const(bgk.shape),
                  const(gnw.shape)],
        out_specs=pl.BlockSpec((2, nb, L, GLA_WIDTH), lambda s: (s, 0, 0, 0)),
        out_shape=jax.ShapeDtypeStruct((n_blocks + 2, nb, L, GLA_WIDTH), BF16),
        scratch_shapes=[pltpu.VMEM((nb, GLA_HEADS * GLA_DK, GLA_HEADS * GLA_DV), F32),
                        pltpu.VMEM((nb * L, wg.shape[1] + GLA_KEY_DIM), F32),
                        pltpu.VMEM((nb * L, wg.shape[1] + GLA_KEY_DIM), F32),
                        pltpu.VMEM((D_MODEL, GLA_KEY_DIM), BF16)],
        compiler_params=pltpu.CompilerParams(
            dimension_semantics=("arbitrary",), vmem_limit_bytes=VMEM_LIMIT_BYTES),
        name="gla_mixer",
    )(x, x, wg, wgl, wup, bgk, gnw)


def _gelu_tanh(y):
    return 0.5 * y * (1.0 + jnp.tanh(np.float32(math.sqrt(2.0 / math.pi)) * (y + 0.044715 * (y * y * y))))


def _s5_kernel(x_ref, wu_ref, win_ref, wintra_ref, wout_ref, ajr_ref, aji_ref, wglu_ref, bglu_ref,
               o_ref, u_ref, st_ref, carry_ref, y_ref, oscr_ref):
    bsz, L, _ = x_ref.shape
    J, P = S5_CHUNK, S5_PITCH
    n_c = L // J
    n_k = S5_WIDTH // LANES
    half = LANES // 2
    hs_states = 2 * S5_STATES // (2 * n_k)
    q_pairs = hs_states // 2 // LANES
    assert J * half == MXU_DIM and J % 2 == 0
    low = lax.broadcasted_iota(jnp.int32, (1, LANES), 1) < half
    swap = lambda t: pltpu.roll(t, half, axis=1)

    @pl.when(pl.program_id(0) == 0)
    def _():
        carry_ref[...] = jnp.zeros_like(carry_ref)

    h = _rms_scale(x_ref[...].reshape(bsz * L, D_MODEL))
    u = jnp.dot(h.astype(BF16), wu_ref[...], preferred_element_type=F32)
    for bb in range(bsz):
        for k in range(n_k):
            u_ref[k, bb * P:bb * P + L, :] = u[bb * L:(bb + 1) * L, k * LANES:(k + 1) * LANES]

    for k in range(n_k):
        lhs = [[], []]
        for c in range(n_c):
            pieces = [u_ref[k, pl.ds(c * J + i, bsz, stride=P), :] for i in range(J)]
            turned = [swap(p) for p in pieces]
            lhs[0].append(jnp.concatenate(
                [jnp.where(low, pieces[i], turned[i + 1]) for i in range(0, J, 2)], axis=1))
            lhs[1].append(jnp.concatenate(
                [jnp.where(low, turned[i], pieces[i + 1]) for i in range(0, J, 2)], axis=1))

        for hh in range(2):
            hs = 2 * k + hh
            s0 = hs * hs_states
            utb = jnp.concatenate(lhs[hh], axis=0).astype(BF16)
            st_ref[:, s0:s0 + hs_states] = jnp.dot(utb, win_ref[hs], preferred_element_type=F32)
            y_ref[hs] = jnp.dot(utb, wintra_ref[hs], preferred_element_type=F32)

            for q in range(q_pairs):
                lr = s0 + q * LANES
                li = lr + hs_states // 2
                ar = jnp.broadcast_to(ajr_ref[hs:hs + 1, q * LANES:(q + 1) * LANES], (bsz, LANES))
                ai = jnp.broadcast_to(aji_ref[hs:hs + 1, q * LANES:(q + 1) * LANES], (bsz, LANES))
                xr = carry_ref[:, lr:lr + LANES]
                xi = carry_ref[:, li:li + LANES]
                for c in range(n_c):
                    rows = slice(c * bsz, (c + 1) * bsz)
                    sr = st_ref[rows, lr:lr + LANES]
                    si = st_ref[rows, li:li + LANES]
                    st_ref[rows, lr:lr + LANES] = xr
                    st_ref[rows, li:li + LANES] = xi
                    xr, xi = ar * xr - ai * xi + sr, ar * xi + ai * xr + si
                carry_ref[:, lr:lr + LANES] = xr
                carry_ref[:, li:li + LANES] = xi

            xs = st_ref[:, s0:s0 + hs_states].astype(BF16)
            y_ref[hs] = _gelu_tanh(y_ref[hs] + jnp.dot(xs, wout_ref[hs], preferred_element_type=F32))

    for r in range(J):
        v = (r // 2) * LANES
        slabs = []
        for k in range(n_k):
            ya, yb = y_ref[2 * k, :, v:v + LANES], y_ref[2 * k + 1, :, v:v + LANES]
            slabs.append(jnp.where(low, ya, swap(yb)) if r % 2 == 0 else jnp.where(low, swap(ya), yb))
        z = jnp.concatenate(slabs, axis=1)
        gate = jnp.dot(z.astype(BF16), wglu_ref[...], preferred_element_type=F32) + bglu_ref[...]
        o_r = z * (1.0 / (1.0 + jnp.exp(-gate)))
        for c in range(n_c):
            for k in range(n_k):
                oscr_ref[k, pl.ds(c * J + r, bsz, stride=P), :] = o_r[c * bsz:(c + 1) * bsz, k * LANES:(k + 1) * LANES]
    for bb in range(bsz):
        for k in range(n_k):
            o_ref[bb, :, k * LANES:(k + 1) * LANES] = oscr_ref[k, bb * P:bb * P + L, :].astype(o_ref.dtype)


def _s5_call(x, wu, win, wintra, wout, ajr, aji, wglu, bglu):
    bsz, seq, _ = x.shape
    L, J = S5_BLOCK, S5_CHUNK
    rows_c = (L // J) * bsz
    n_k = S5_WIDTH // LANES
    const = lambda shape: pl.BlockSpec(shape, lambda t: (0,) * len(shape))
    return pl.pallas_call(
        _s5_kernel,
        grid=(seq // L,),
        in_specs=[pl.BlockSpec((bsz, L, D_MODEL), lambda t: (0, t, 0))]
        + [const(a.shape) for a in (wu, win, wintra, wout, ajr, aji, wglu, bglu)],
        out_specs=pl.BlockSpec((bsz, L, S5_WIDTH), lambda t: (0, t, 0)),
        out_shape=jax.ShapeDtypeStruct((bsz, seq, S5_WIDTH), BF16),
        scratch_shapes=[
            pltpu.VMEM((n_k, bsz * S5_PITCH, LANES), F32),
            pltpu.VMEM((rows_c, 2 * S5_STATES), F32),
            pltpu.VMEM((bsz, 2 * S5_STATES), F32),
            pltpu.VMEM((2 * n_k, rows_c, J * LANES // 2), F32),
            pltpu.VMEM((n_k, bsz * S5_PITCH, LANES), F32),
        ],
        compiler_params=pltpu.CompilerParams(
            dimension_semantics=("arbitrary",), vmem_limit_bytes=VMEM_LIMIT_BYTES),
        name="s5_mixer",
    )(x, wu, win, wintra, wout, ajr, aji, wglu, bglu)


def _s5_prepare(a_re, a_im, log_dt, b_re, b_im, c_re, c_im, d_skip):
    J = S5_CHUNK
    width = LANES // 2
    n_k = S5_WIDTH // width
    gpk = S5_GROUPS // n_k
    lanes = gpk * S5_STATE
    hi = lax.Precision.HIGHEST

    per_state = lambda t: t.reshape(n_k, 1, lanes)
    ar, ai = per_state(a_re), per_state(a_im)
    dt = per_state(jnp.broadcast_to(jnp.exp(log_dt)[:, None], a_re.shape))
    b_t = lambda t: t.reshape(n_k, gpk, S5_STATE, S5_GROUP).transpose(0, 3, 1, 2).reshape(n_k, S5_GROUP, lanes)
    c_t = lambda t: t.reshape(n_k, gpk, S5_GROUP, S5_STATE).transpose(0, 2, 1, 3).reshape(n_k, S5_GROUP, lanes)
    b_re, b_im, c_re, c_im = b_t(b_re), b_t(b_im), c_t(c_re), c_t(c_im)

    def apow(m):
        mag = jnp.exp(m * ar * dt)
        return mag * jnp.cos(m * ai * dt), mag * jnp.sin(m * ai * dt)

    abr, abi = apow(1)
    den = ar * ar + ai * ai
    nr = abr - 1.0
    fr = (nr * ar + abi * ai) / den
    fi = (abi * ar - nr * ai) / den
    bbr = fr * b_re - fi * b_im
    bbi = fr * b_im + fi * b_re

    ab_re, ab_im = [], []
    for m in range(J):
        pr, pi = apow(m)
        ab_re.append(pr * bbr - pi * bbi)
        ab_im.append(pr * bbi + pi * bbr)

    def block_diag(t, lane_period, lane_group_width):
        n_lanes = t.shape[-1]
        row_g = lax.broadcasted_iota(jnp.int32, (gpk, 1, n_lanes), 0)
        lane_g = (lax.broadcasted_iota(jnp.int32, (gpk, 1, n_lanes), 2) % lane_period) // lane_group_width
        full = jnp.where(row_g == lane_g, t[:, :, None, :, :], 0.0)
        return full.reshape(n_k, J * width, n_lanes)

    win = block_diag(jnp.stack(
        [jnp.concatenate([ab_re[J - 1 - i], ab_im[J - 1 - i]], axis=-1) for i in range(J)], axis=1),
        lanes, S5_STATE)

    t_out = []
    for r in range(J):
        pr, pi = apow(r + 1)
        t_out.append(jnp.concatenate([c_re * pr - c_im * pi, -(c_re * pi + c_im * pr)], axis=-1))
    wout = block_diag(jnp.stack(t_out, axis=1), lanes, S5_STATE).transpose(0, 2, 1)

    grp = lambda t: t.reshape(n_k, S5_GROUP, gpk, S5_STATE)
    kmat = [(jnp.einsum('kqgn,kpgn->kpgq', grp(c_re), grp(ab_re[m]), precision=hi)
             - jnp.einsum('kqgn,kpgn->kpgq', grp(c_im), grp(ab_im[m]), precision=hi)).reshape(n_k, S5_GROUP, width)
            for m in range(J)]
    d_t = d_skip.reshape(n_k, gpk, S5_GROUP).transpose(0, 2, 1)
    p_eq_q = (lax.broadcasted_iota(jnp.int32, (S5_GROUP, gpk, S5_GROUP), 0)
              == lax.broadcasted_iota(jnp.int32, (S5_GROUP, gpk, S5_GROUP), 2))
    kmat[0] = kmat[0] + jnp.where(p_eq_q, d_t[..., None], 0.0).reshape(n_k, S5_GROUP, width)
    zero = jnp.zeros_like(kmat[0])
    t_intra = jnp.stack([jnp.concatenate([kmat[r - i] if i <= r else zero for r in range(J)], axis=-1)
                         for i in range(J)], axis=1)
    wintra = block_diag(t_intra, width, S5_GROUP)

    ajr, aji = apow(J)
    return (win.astype(BF16), wintra.astype(BF16), wout.astype(BF16), ajr.reshape(n_k, lanes), aji.reshape(n_k, lanes))


def _out_mlp_kernel(n_og, x_ref, *refs):
    og_refs = refs[:n_og]
    os_ref, wo_ref, wup_ref, wdn_ref, fw_ref, o_ref = refs[n_og:]
    og = jnp.concatenate([og_ref[...] for og_ref in og_refs], axis=0)
    x1 = x_ref[...] + jnp.dot(og, wo_ref[0:GLA_WIDTH, :], preferred_element_type=F32) \
        + jnp.dot(os_ref[...], wo_ref[GLA_WIDTH:, :], preferred_element_type=F32)
    h = _rms_scale(x1).astype(BF16)
    acts = []
    for c in range(D_FF // MLP_FF_CHUNK):
        cs = slice(c * MLP_FF_CHUNK, (c + 1) * MLP_FF_CHUNK)
        a = jnp.maximum(jnp.dot(h, wup_ref[:, cs], preferred_element_type=F32), 0.0)
        acts.append((a * a).astype(BF16))
    act = jnp.concatenate(acts, axis=1)
    for r0 in range(0, MLP_ROWS, MLP_DOWN_ROWS):
        r = slice(r0, r0 + MLP_DOWN_ROWS)
        x2 = x1[r] + jnp.dot(act[r], wdn_ref[...], preferred_element_type=F32)
        o_ref[r, :] = _rmsnorm(x2, fw_ref[...])


def _out_mlp_call(x2, og_blocks, os_, wo, wup, wdn, fw, seq):
    rows = x2.shape[0]
    nb, L = og_blocks.shape[1], og_blocks.shape[2]
    assert MLP_ROWS % L == 0 and seq % MLP_ROWS == 0
    n_og = MLP_ROWS // L
    tiles_per_seq = seq // MLP_ROWS
    const = lambda shape: pl.BlockSpec(shape, lambda i: (0,) * len(shape), pipeline_mode=pl.Buffered(1))
    tile = lambda width: pl.BlockSpec((MLP_ROWS, width), lambda i: (i, 0))

    def og_spec(e):
        def index(i):
            b, t = i // tiles_per_seq, i % tiles_per_seq
            return ((b // nb) * (seq // L) + n_og * t + e + 1, b % nb, 0, 0)
        return pl.BlockSpec((None, None, L, GLA_WIDTH), index)

    return pl.pallas_call(
        functools.partial(_out_mlp_kernel, n_og),
        grid=(rows // MLP_ROWS,),
        in_specs=[tile(D_MODEL)] + [og_spec(e) for e in range(n_og)] + [tile(S5_WIDTH)]
        + [const(a.shape) for a in (wo, wup, wdn, fw)],
        out_specs=tile(D_MODEL),
        out_shape=jax.ShapeDtypeStruct((rows, D_MODEL), F32),
        compiler_params=pltpu.CompilerParams(
            dimension_semantics=("arbitrary",), vmem_limit_bytes=VMEM_LIMIT_BYTES),
        name="out_mlp",
    )(x2, *([og_blocks] * n_og), os_, wo, wup, wdn, fw)


def kernel(x, norm_mix_w, w_in, w_gk_up, b_gk, gla_norm_w, s5_a_re, s5_a_im, s5_log_dt, s5_b_re, s5_b_im,
           s5_c_re, s5_c_im, s5_d, w_glu, b_glu, w_out, norm_mlp_w, w_mlp_up, w_mlp_down, norm_final_w):
    bsz, seq, _ = x.shape
    assert bsz % GLA_SEQS == 0 and seq % GLA_BLOCK == 0 and seq % S5_BLOCK == 0 and (bsz * seq) % MLP_ROWS == 0
    assert norm_mix_w.shape[0] == 1, "single-layer problem"
    for l in range(1):
        w = w_in[l] * norm_mix_w[l][:, None]
        kd2, r = 2 * GLA_KEY_DIM, GLA_GATE_RANK
        wg = jnp.concatenate([w[:, 0:kd2 + GLA_WIDTH], w[:, kd2 + GLA_WIDTH + r:kd2 + 2 * GLA_WIDTH + r]],
                             axis=1).astype(BF16)
        wgl = jnp.concatenate([w[:, kd2 + GLA_WIDTH:kd2 + GLA_WIDTH + r], jnp.zeros((D_MODEL, LANES - r), F32)], axis=1)
        wu = w[:, kd2 + 2 * GLA_WIDTH + r:].astype(BF16)
        wup = jnp.concatenate([w_gk_up[l], jnp.zeros((LANES - r, GLA_KEY_DIM), F32)], axis=0)

        o_gla = _gla_call(x, wg, wgl, wup, b_gk[l][None, :], gla_norm_w[l][None, :])

        win, wintra, wout, ajr, aji = _s5_prepare(s5_a_re[l], s5_a_im[l], s5_log_dt[l], s5_b_re[l], s5_b_im[l],
                                                  s5_c_re[l], s5_c_im[l], s5_d[l])
        o_s5 = _s5_call(x, wu, win, wintra, wout, ajr, aji, w_glu[l].astype(BF16), b_glu[l][None, :])

        rows = bsz * seq
        out = _out_mlp_call(x.reshape(rows, D_MODEL), o_gla, o_s5.reshape(rows, S5_WIDTH),
                            w_out[l].astype(BF16), (w_mlp_up[l] * norm_mlp_w[l][:, None]).astype(BF16),
                            w_mlp_down[l].astype(BF16), norm_final_w[None, :], seq)
        x = out.reshape(bsz, seq, D_MODEL)
    return x
```

```python
import functools
import math

import jax
import jax.numpy as jnp
import numpy as np
from jax import lax
from jax.experimental import pallas as pl
from jax.experimental.pallas import tpu as pltpu

F32 = jnp.float32
BF16 = jnp.bfloat16

D_MODEL = 1024
GLA_WIDTH = 512
S5_WIDTH = 512
GLA_HEADS = 4
GLA_KEY_DIM = 256
GLA_DK = 64
GLA_DV = 128
GLA_GATE_RANK = 16
GLA_GATE_NORMALIZER = 16.0
GLA_CHUNK = 64
GLA_SPAN = 2 * GLA_CHUNK
S5_GROUP = 16
S5_GROUPS = 32
S5_STATE = 64
D_FF = 4096
EPS = 1e-6

LANES = 128
MXU_DIM = 256
S5_STATES = S5_GROUPS * S5_STATE
VMEM_LIMIT_BYTES = 56 * 1024 * 1024

GLA_BLOCK = 256
GLA_SEQS = 2
S5_BLOCK = 64
S5_CHUNK = 4
S5_PITCH = S5_BLOCK + 8
MLP_ROWS = 1024
MLP_FF_CHUNK = 1024
MLP_DOWN_ROWS = 256


def _rms_scale(x):
    return x * lax.rsqrt(jnp.mean(x * x, axis=-1, keepdims=True) + EPS)


def _rmsnorm(x, w):
    return _rms_scale(x) * w


def _bf16_parts(x, n):
    parts = []
    for _ in range(n):
        part = x.astype(BF16)
        parts.append(part)
        x = x - part.astype(F32)
    return parts


def _gla_substep(x_ref, p_new_ref, p_ref, st_ref, o_ref, first, wg_ref, wgk_ref, bgk_ref, gnw_ref,
                 tri, causal_heads, key_head, value_head, same_head, span_of_t):
    nb, L, _ = x_ref.shape
    n_spans = L // GLA_SPAN
    seqs = range(nb)
    cols = lambda i, c0, c1: p_ref[i * L:(i + 1) * L, c0:c1]

    hb = _rms_scale(x_ref[...].reshape(nb * L, D_MODEL)).astype(BF16)
    n_cols = wg_ref.shape[1]
    tiles = [(wg_ref, c0, c0) for c0 in range(0, n_cols, MXU_DIM)] + [(wgk_ref, 0, n_cols)]

    def project_tile():
        if tiles:
            w_ref, c0, dst = tiles.pop(0)
            p_new_ref[:, dst:dst + MXU_DIM] = jnp.dot(hb, w_ref[:, c0:c0 + MXU_DIM], preferred_element_type=F32)

    log_a = []
    for i in seqs:
        gk = cols(i, n_cols, n_cols + GLA_KEY_DIM) + bgk_ref[...]
        log_a.append((jnp.minimum(gk, 0.0) - jnp.log1p(jnp.exp(-jnp.abs(gk)))) * (1.0 / GLA_GATE_NORMALIZER))
    project_tile()

    b = []
    for i in seqs:
        terms = jnp.dot(tri, jnp.concatenate(_bf16_parts(log_a[i], 3), axis=1), preferred_element_type=F32)
        b.append(sum(terms[:, j * GLA_KEY_DIM:(j + 1) * GLA_KEY_DIM] for j in range(3)))
    span_row = lambda t, r: jnp.concatenate(
        [jnp.broadcast_to(t[m * GLA_SPAN + r:m * GLA_SPAN + r + 1, :], (GLA_SPAN, GLA_KEY_DIM))
         for m in range(n_spans)], axis=0)
    b_mid = [span_row(b[i], GLA_CHUNK - 1) for i in seqs]
    b_last = [span_row(b[i], GLA_SPAN - 1) for i in seqs]
    project_tile()

    q = [cols(i, 0, 256) * (GLA_DK ** -0.5) for i in seqs]
    q_s = [q[i] * jnp.exp(b[i] - b_mid[i]) for i in seqs]
    q_b = [(q[i] * jnp.exp(b[i])).astype(BF16) for i in seqs]
    k_et = [(cols(i, 256, 512) * jnp.exp(b_mid[i] - b[i])).T.astype(BF16) for i in seqs]
    k_dt = [(cols(i, 256, 512) * jnp.exp(b_last[i] - b[i])).T.astype(BF16) for i in seqs]
    b_t = [b[i].T for i in seqs]
    v_b = [cols(i, 512, 1024).astype(BF16) for i in seqs]
    project_tile()

    pair = MXU_DIM // GLA_DV
    q_sb = [q_s[i].astype(BF16) for i in seqs]
    o_intra = [[] for _ in seqs]
    for m in range(n_spans):
        r0, r1 = m * GLA_SPAN, (m + 1) * GLA_SPAN
        for i in seqs:
            k_span = k_et[i][:, r0:r1]
            k_blk = jnp.concatenate([jnp.where(key_head == hh, k_span, jnp.zeros_like(k_span))
                                     for hh in range(GLA_HEADS)], axis=1)
            sc = jnp.dot(q_sb[i][r0:r1], k_blk, preferred_element_type=F32)
            sc = jnp.where(causal_heads, sc, 0.0).astype(BF16)
            outs = []
            for g in range(GLA_HEADS // pair):
                v_pair = v_b[i][r0:r1, g * MXU_DIM:(g + 1) * MXU_DIM]
                v_blk = jnp.concatenate([jnp.where(value_head == j, v_pair, jnp.zeros_like(v_pair))
                                         for j in range(pair)], axis=0)
                outs.append(jnp.dot(sc[:, g * pair * GLA_SPAN:(g + 1) * pair * GLA_SPAN], v_blk,
                                    preferred_element_type=F32))
            o_intra[i].append(jnp.concatenate(outs, axis=1))
        project_tile()
        project_tile()
        project_tile()

    st = [jnp.where(first, 0.0, st_ref[i]) for i in seqs]
    o_inter = [[] for _ in seqs]
    for m in range(n_spans):
        r0, r1 = m * GLA_SPAN, (m + 1) * GLA_SPAN
        for i in seqs:
            o_inter[i].append(jnp.dot(q_b[i][r0:r1], st[i].astype(BF16), preferred_element_type=F32))
            k_m = jnp.where(span_of_t == m, k_dt[i], jnp.zeros_like(k_dt[i]))
            upd = jnp.dot(k_m, v_b[i], preferred_element_type=F32)
            decay = jnp.exp(b_t[i][:, r1 - 1:r1])
            st[i] = jnp.where(same_head, st[i] * decay + upd, 0.0)
        project_tile()
        project_tile()

    for i in seqs:
        st_ref[i] = st[i]
        o = jnp.concatenate(o_intra[i], axis=0) + jnp.concatenate(o_inter[i], axis=0)
        outs = []
        for hh in range(GLA_HEADS):
            oh = o[:, hh * GLA_DV:(hh + 1) * GLA_DV]
            outs.append(oh * lax.rsqrt(jnp.mean(oh * oh, axis=-1, keepdims=True) + EPS) * gnw_ref[...])
        g = cols(i, 1024, 1536)
        o_ref[i] = (jnp.concatenate(outs, axis=1) * (g * (1.0 / (1.0 + jnp.exp(-g))))).astype(o_ref.dtype)
    while tiles:
        project_tile()


def _gla_kernel(blocks_per_seq, xa_ref, xb_ref, wg_ref, wgl_ref, wup_ref, bgk_ref, gnw_ref, o_ref,
                st_ref, pa_ref, pb_ref, wgk_ref):
    s = pl.program_id(0)
    L = xa_ref.shape[1]

    @pl.when(s == 0)
    def _():
        st_ref[...] = jnp.zeros_like(st_ref)
        pb_ref[...] = jnp.zeros_like(pb_ref)
        wgk_ref[...] = jnp.dot(wgl_ref[...], wup_ref[...], precision=lax.Precision.HIGHEST,
                               preferred_element_type=F32).astype(BF16)

    row = lax.broadcasted_iota(jnp.int32, (L, L), 0)
    col = lax.broadcasted_iota(jnp.int32, (L, L), 1)
    causal = ((row // GLA_SPAN) == (col // GLA_SPAN)) & (col <= row)
    tri = jnp.where(causal, 1.0, 0.0).astype(BF16)
    srow = lax.broadcasted_iota(jnp.int32, st_ref.shape[1:], 0) // GLA_DK
    scol = lax.broadcasted_iota(jnp.int32, st_ref.shape[1:], 1) // GLA_DV
    same_head = srow == scol
    span_of_t = lax.broadcasted_iota(jnp.int32, (GLA_KEY_DIM, L), 1) // GLA_SPAN
    srow = lax.broadcasted_iota(jnp.int32, (GLA_SPAN, GLA_HEADS * GLA_SPAN), 0)
    scol = lax.broadcasted_iota(jnp.int32, (GLA_SPAN, GLA_HEADS * GLA_SPAN), 1) % GLA_SPAN
    causal_heads = scol <= srow
    key_head = lax.broadcasted_iota(jnp.int32, (GLA_KEY_DIM, GLA_SPAN), 0) // GLA_DK
    value_head = lax.broadcasted_iota(jnp.int32, (GLA_SPAN, MXU_DIM), 1) // GLA_DV
    shared = (wg_ref, wgk_ref, bgk_ref, gnw_ref, tri, causal_heads, key_head, value_head, same_head, span_of_t)

    _gla_substep(xa_ref, pa_ref, pb_ref, st_ref, o_ref.at[0], (2 * s - 1) % blocks_per_seq == 0, *shared)
    _gla_substep(xb_ref, pb_ref, pa_ref, st_ref, o_ref.at[1], (2 * s) % blocks_per_seq == 0, *shared)


def _gla_call(x, wg, wgl, wup, bgk, gnw):
    bsz, seq, _ = x.shape
    nb, L = GLA_SEQS, GLA_BLOCK
    bps = seq // L
    n_blocks = (bsz // nb) * bps
    assert n_blocks % 2 == 0 and bps % 2 == 0
    const = lambda shape: pl.BlockSpec(shape, lambda s: (0,) * len(shape))

    def x_spec(e):
        def index(s):
            n = jnp.minimum(2 * s + e, n_blocks - 1)
            return (n // bps, n % bps, 0)
        return pl.BlockSpec((nb, L, D_MODEL), index)

    return pl.pallas_call(
        functools.partial(_gla_kernel, bps),
        grid=(n_blocks // 2 + 1,),
        in_specs=[x_spec(0), x_spec(1), const(wg.shape), const(wgl.shape), const(wup.shape), const(bgk.shape),
                  const(gnw.shape)],
        out_specs=pl.BlockSpec((2, nb, L, GLA_WIDTH), lambda s: (s, 0, 0, 0)),
        out_shape=jax.ShapeDtypeStruct((n_blocks + 2, nb, L, GLA_WIDTH), BF16),
        scratch_shapes=[pltpu.VMEM((nb, GLA_HEADS * GLA_DK, GLA_HEADS * GLA_DV), F32),
                        pltpu.VMEM((nb * L, wg.shape[1] + GLA_KEY_DIM), F32),
                        pltpu.VMEM((nb * L, wg.shape[1] + GLA_KEY_DIM), F32),
                        pltpu.VMEM((D_MODEL, GLA_KEY_DIM), BF16)],
        compiler_params=pltpu.CompilerParams(
            dimension_semantics=("arbitrary",), vmem_limit_bytes=VMEM_LIMIT_BYTES),
        name="gla_mixer",
    )(x, x, wg, wgl, wup, bgk, gnw)


def _gelu_tanh(y):
    return 0.5 * y * (1.0 + jnp.tanh(np.float32(math.sqrt(2.0 / math.pi)) * (y + 0.044715 * (y * y * y))))


def _s5_kernel(x_ref, wu_ref, win_ref, wintra_ref, wout_ref, ajr_ref, aji_ref, wglu_ref, bglu_ref,
               o_ref, u_ref, st_ref, carry_ref, y_ref, oscr_ref):
    bsz, L, _ = x_ref.shape
    J, P = S5_CHUNK, S5_PITCH
    n_c = L // J
    n_k = S5_WIDTH // LANES
    half = LANES // 2
    hs_states = 2 * S5_STATES // (2 * n_k)
    q_pairs = hs_states // 2 // LANES
    assert J * half == MXU_DIM and J % 2 == 0
    low = lax.broadcasted_iota(jnp.int32, (1, LANES), 1) < half
    swap = lambda t: pltpu.roll(t, half, axis=1)

    @pl.when(pl.program_id(0) == 0)
    def _():
        carry_ref[...] = jnp.zeros_like(carry_ref)

    h = _rms_scale(x_ref[...].reshape(bsz * L, D_MODEL))
    u = jnp.dot(h.astype(BF16), wu_ref[...], preferred_element_type=F32)
    for bb in range(bsz):
        for k in range(n_k):
            u_ref[k, bb * P:bb * P + L, :] = u[bb * L:(bb + 1) * L, k * LANES:(k + 1) * LANES]

    for k in range(n_k):
        lhs = [[], []]
        for c in range(n_c):
            pieces = [u_ref[k, pl.ds(c * J + i, bsz, stride=P), :] for i in range(J)]
            turned = [swap(p) for p in pieces]
            lhs[0].append(jnp.concatenate(
                [jnp.where(low, pieces[i], turned[i + 1]) for i in range(0, J, 2)], axis=1))
            lhs[1].append(jnp.concatenate(
                [jnp.where(low, turned[i], pieces[i + 1]) for i in range(0, J, 2)], axis=1))

        for hh in range(2):
            hs = 2 * k + hh
            s0 = hs * hs_states
            utb = jnp.concatenate(lhs[hh], axis=0).astype(BF16)
            st_ref[:, s0:s0 + hs_states] = jnp.dot(utb, win_ref[hs], preferred_element_type=F32)
            y_ref[hs] = jnp.dot(utb, wintra_ref[hs], preferred_element_type=F32)

            for q in range(q_pairs):
                lr = s0 + q * LANES
                li = lr + hs_states // 2
                ar = jnp.broadcast_to(ajr_ref[hs:hs + 1, q * LANES:(q + 1) * LANES], (bsz, LANES))
                ai = jnp.broadcast_to(aji_ref[hs:hs + 1, q * LANES:(q + 1) * LANES], (bsz, LANES))
                xr = carry_ref[:, lr:lr + LANES]
                xi = carry_ref[:, li:li + LANES]
                for c in range(n_c):
                    rows = slice(c * bsz, (c + 1) * bsz)
                    sr = st_ref[rows, lr:lr + LANES]
                    si = st_ref[rows, li:li + LANES]
                    st_ref[rows, lr:lr + LANES] = xr
                    st_ref[rows, li:li + LANES] = xi
                    xr, xi = ar * xr - ai * xi + sr, ar * xi + ai * xr + si
                carry_ref[:, lr:lr + LANES] = xr
                carry_ref[:, li:li + LANES] = xi

            xs = st_ref[:, s0:s0 + hs_states].astype(BF16)
            y_ref[hs] = _gelu_tanh(y_ref[hs] + jnp.dot(xs, wout_ref[hs], preferred_element_type=F32))

    for r in range(J):
        v = (r // 2) * LANES
        slabs = []
        for k in range(n_k):
            ya, yb = y_ref[2 * k, :, v:v + LANES], y_ref[2 * k + 1, :, v:v + LANES]
            slabs.append(jnp.where(low, ya, swap(yb)) if r % 2 == 0 else jnp.where(low, swap(ya), yb))
        z = jnp.concatenate(slabs, axis=1)
        gate = jnp.dot(z.astype(BF16), wglu_ref[...], preferred_element_type=F32) + bglu_ref[...]
        o_r = z * (1.0 / (1.0 + jnp.exp(-gate)))
        for c in range(n_c):
            for k in range(n_k):
                oscr_ref[k, pl.ds(c * J + r, bsz, stride=P), :] = o_r[c * bsz:(c + 1) * bsz, k * LANES:(k + 1) * LANES]
    for bb in range(bsz):
        for k in range(n_k):
            o_ref[bb, :, k * LANES:(k + 1) * LANES] = oscr_ref[k, bb * P:bb * P + L, :].astype(o_ref.dtype)


def _s5_call(x, wu, win, wintra, wout, ajr, aji, wglu, bglu):
    bsz, seq, _ = x.shape
    L, J = S5_BLOCK, S5_CHUNK
    rows_c = (L // J) * bsz
    n_k = S5_WIDTH // LANES
    const = lambda shape: pl.BlockSpec(shape, lambda t: (0,) * len(shape))
    return pl.pallas_call(
        _s5_kernel,
        grid=(seq // L,),
        in_specs=[pl.BlockSpec((bsz, L, D_MODEL), lambda t: (0, t, 0))]
        + [const(a.shape) for a in (wu, win, wintra, wout, ajr, aji, wglu, bglu)],
        out_specs=pl.BlockSpec((bsz, L, S5_WIDTH), lambda t: (0, t, 0)),
        out_shape=jax.ShapeDtypeStruct((bsz, seq, S5_WIDTH), BF16),
        scratch_shapes=[
            pltpu.VMEM((n_k, bsz * S5_PITCH, LANES), F32),
            pltpu.VMEM((rows_c, 2 * S5_STATES), F32),
            pltpu.VMEM((bsz, 2 * S5_STATES), F32),
            pltpu.VMEM((2 * n_k, rows_c, J * LANES // 2), F32),
            pltpu.VMEM((n_k, bsz * S5_PITCH, LANES), F32),
        ],
        compiler_params=pltpu.CompilerParams(
            dimension_semantics=("arbitrary",), vmem_limit_bytes=VMEM_LIMIT_BYTES),
        name="s5_mixer",
    )(x, wu, win, wintra, wout, ajr, aji, wglu, bglu)


def _s5_prepare(a_re, a_im, log_dt, b_re, b_im, c_re, c_im, d_skip):
    J = S5_CHUNK
    width = LANES // 2
    n_k = S5_WIDTH // width
    gpk = S5_GROUPS // n_k
    lanes = gpk * S5_STATE
    hi = lax.Precision.HIGHEST

    per_state = lambda t: t.reshape(n_k, 1, lanes)
    ar, ai = per_state(a_re), per_state(a_im)
    dt = per_state(jnp.broadcast_to(jnp.exp(log_dt)[:, None], a_re.shape))
    b_t = lambda t: t.reshape(n_k, gpk, S5_STATE, S5_GROUP).transpose(0, 3, 1, 2).reshape(n_k, S5_GROUP, lanes)
    c_t = lambda t: t.reshape(n_k, gpk, S5_GROUP, S5_STATE).transpose(0, 2, 1, 3).reshape(n_k, S5_GROUP, lanes)
    b_re, b_im, c_re, c_im = b_t(b_re), b_t(b_im), c_t(c_re), c_t(c_im)

    def apow(m):
        mag = jnp.exp(m * ar * dt)
        return mag * jnp.cos(m * ai * dt), mag * jnp.sin(m * ai * dt)

    abr, abi = apow(1)
    den = ar * ar + ai * ai
    nr = abr - 1.0
    fr = (nr * ar + abi * ai) / den
    fi = (abi * ar - nr * ai) / den
    bbr = fr * b_re - fi * b_im
    bbi = fr * b_im + fi * b_re

    ab_re, ab_im = [], []
    for m in range(J):
        pr, pi = apow(m)
        ab_re.append(pr * bbr - pi * bbi)
        ab_im.append(pr * bbi + pi * bbr)

    def block_diag(t, lane_period, lane_group_width):
        n_lanes = t.shape[-1]
        row_g = lax.broadcasted_iota(jnp.int32, (gpk, 1, n_lanes), 0)
        lane_g = (lax.broadcasted_iota(jnp.int32, (gpk, 1, n_lanes), 2) % lane_period) // lane_group_width
        full = jnp.where(row_g == lane_g, t[:, :, None, :, :], 0.0)
        return full.reshape(n_k, J * width, n_lanes)

    win = block_diag(jnp.stack(
        [jnp.concatenate([ab_re[J - 1 - i], ab_im[J - 1 - i]], axis=-1) for i in range(J)], axis=1),
        lanes, S5_STATE)

    t_out = []
    for r in range(J):
        pr, pi = apow(r + 1)
        t_out.append(jnp.concatenate([c_re * pr - c_im * pi, -(c_re * pi + c_im * pr)], axis=-1))
    wout = block_diag(jnp.stack(t_out, axis=1), lanes, S5_STATE).transpose(0, 2, 1)

    grp = lambda t: t.reshape(n_k, S5_GROUP, gpk, S5_STATE)
    kmat = [(jnp.einsum('kqgn,kpgn->kpgq', grp(c_re), grp(ab_re[m]), precision=hi)
             - jnp.einsum('kqgn,kpgn->kpgq', grp(c_im), grp(ab_im[m]), precision=hi)).reshape(n_k, S5_GROUP, width)
            for m in range(J)]
    d_t = d_skip.reshape(n_k, gpk, S5_GROUP).transpose(0, 2, 1)
    p_eq_q = (lax.broadcasted_iota(jnp.int32, (S5_GROUP, gpk, S5_GROUP), 0)
              == lax.broadcasted_iota(jnp.int32, (S5_GROUP, gpk, S5_GROUP), 2))
    kmat[0] = kmat[0] + jnp.where(p_eq_q, d_t[..., None], 0.0).reshape(n_k, S5_GROUP, width)
    zero = jnp.zeros_like(kmat[0])
    t_intra = jnp.stack([jnp.concatenate([kmat[r - i] if i <= r else zero for r in range(J)], axis=-1)
                         for i in range(J)], axis=1)
    wintra = block_diag(t_intra, width, S5_GROUP)

    ajr, aji = apow(J)
    return (win.astype(BF16), wintra.astype(BF16), wout.astype(BF16), ajr.reshape(n_k, lanes), aji.reshape(n_k, lanes))


def _out_mlp_kernel(n_og, x_ref, *refs):
    og_refs = refs[:n_og]
    os_ref, wo_ref, wup_ref, wdn_ref, fw_ref, o_ref = refs[n_og:]
    og = jnp.concatenate([og_ref[...] for og_ref in og_refs], axis=0)
    x1 = x_ref[...] + jnp.dot(og, wo_ref[0:GLA_WIDTH, :], preferred_element_type=F32) \
        + jnp.dot(os_ref[...], wo_ref[GLA_WIDTH:, :], preferred_element_type=F32)
    h = _rms_scale(x1).astype(BF16)
    acts = []
    for c in range(D_FF // MLP_FF_CHUNK):
        cs = slice(c * MLP_FF_CHUNK, (c + 1) * MLP_FF_CHUNK)
        a = jnp.maximum(jnp.dot(h, wup_ref[:, cs], preferred_element_type=F32), 0.0)
        acts.append((a * a).astype(BF16))
    act = jnp.concatenate(acts, axis=1)
    for r0 in range(0, MLP_ROWS, MLP_DOWN_ROWS):
        r = slice(r0, r0 + MLP_DOWN_ROWS)
        x2 = x1[r] + jnp.dot(act[r], wdn_ref[...], preferred_element_type=F32)
        o_ref[r, :] = _rmsnorm(x2, fw_ref[...])


def _out_mlp_call(x2, og_blocks, os_, wo, wup, wdn, fw, seq):
    rows = x2.shape[0]
    nb, L = og_blocks.shape[1], og_blocks.shape[2]
    assert MLP_ROWS % L == 0 and seq % MLP_ROWS == 0
    n_og = MLP_ROWS // L
    tiles_per_seq = seq // MLP_ROWS
    const = lambda shape: pl.BlockSpec(shape, lambda i: (0,) * len(shape), pipeline_mode=pl.Buffered(1))
    tile = lambda width: pl.BlockSpec((MLP_ROWS, width), lambda i: (i, 0))

    def og_spec(e):
        def index(i):
            b, t = i // tiles_per_seq, i % tiles_per_seq
            return ((b // nb) * (seq // L) + n_og * t + e + 1, b % nb, 0, 0)
        return pl.BlockSpec((None, None, L, GLA_WIDTH), index)

    return pl.pallas_call(
        functools.partial(_out_mlp_kernel, n_og),
        grid=(rows // MLP_ROWS,),
        in_specs=[tile(D_MODEL)] + [og_spec(e) for e in range(n_og)] + [tile(S5_WIDTH)]
        + [const(a.shape) for a in (wo, wup, wdn, fw)],
        out_specs=tile(D_MODEL),
        out_shape=jax.ShapeDtypeStruct((rows, D_MODEL), F32),
        compiler_params=pltpu.CompilerParams(
            dimension_semantics=("arbitrary",), vmem_limit_bytes=VMEM_LIMIT_BYTES),
        name="out_mlp",
    )(x2, *([og_blocks] * n_og), os_, wo, wup, wdn, fw)


def kernel(x, norm_mix_w, w_in, w_gk_up, b_gk, gla_norm_w, s5_a_re, s5_a_im, s5_log_dt, s5_b_re, s5_b_im,
           s5_c_re, s5_c_im, s5_d, w_glu, b_glu, w_out, norm_mlp_w, w_mlp_up, w_mlp_down, norm_final_w):
    bsz, seq, _ = x.shape
    assert bsz % GLA_SEQS == 0 and seq % GLA_BLOCK == 0 and seq % S5_BLOCK == 0 and (bsz * seq) % MLP_ROWS == 0
    assert norm_mix_w.shape[0] == 1, "single-layer problem"
    for l in range(1):
        w = w_in[l] * norm_mix_w[l][:, None]
        kd2, r = 2 * GLA_KEY_DIM, GLA_GATE_RANK
        wg = jnp.concatenate([w[:, 0:kd2 + GLA_WIDTH], w[:, kd2 + GLA_WIDTH + r:kd2 + 2 * GLA_WIDTH + r]],
                             axis=1).astype(BF16)
        wgl = jnp.concatenate([w[:, kd2 + GLA_WIDTH:kd2 + GLA_WIDTH + r], jnp.zeros((D_MODEL, LANES - r), F32)], axis=1)
        wu = w[:, kd2 + 2 * GLA_WIDTH + r:].astype(BF16)
        wup = jnp.concatenate([w_gk_up[l], jnp.zeros((LANES - r, GLA_KEY_DIM), F32)], axis=0)

        o_gla = _gla_call(x, wg, wgl, wup, b_gk[l][None, :], gla_norm_w[l][None, :])

        win, wintra, wout, ajr, aji = _s5_prepare(s5_a_re[l], s5_a_im[l], s5_log_dt[l], s5_b_re[l], s5_b_im[l],
                                                  s5_c_re[l], s5_c_im[l], s5_d[l])
        o_s5 = _s5_call(x, wu, win, wintra, wout, ajr, aji, w_glu[l].astype(BF16), b_glu[l][None, :])

        rows = bsz * seq
        out = _out_mlp_call(x.reshape(rows, D_MODEL), o_gla, o_s5.reshape(rows, S5_WIDTH),
                            w_out[l].astype(BF16), (w_mlp_up[l] * norm_mlp_w[l][:, None]).astype(BF16),
                            w_mlp_down[l].astype(BF16), norm_final_w[None, :], seq)
        x = out.reshape(bsz, seq, D_MODEL)
    return x
```

```python
import functools
import math

import jax
import jax.numpy as jnp
import numpy as np
from jax import lax
from jax.experimental import pallas as pl
from jax.experimental.pallas import tpu as pltpu

F32 = jnp.float32
BF16 = jnp.bfloat16

D_MODEL = 1024
GLA_WIDTH = 512
S5_WIDTH = 512
GLA_HEADS = 4
GLA_KEY_DIM = 256
GLA_DK = 64
GLA_DV = 128
GLA_GATE_RANK = 16
GLA_GATE_NORMALIZER = 16.0
GLA_CHUNK = 64
GLA_SPAN = 2 * GLA_CHUNK
S5_GROUP = 16
S5_GROUPS = 32
S5_STATE = 64
D_FF = 4096
EPS = 1e-6

LANES = 128
MXU_DIM = 256
S5_STATES = S5_GROUPS * S5_STATE
VMEM_LIMIT_BYTES = 56 * 1024 * 1024

GLA_BLOCK = 256
GLA_SEQS = 2
S5_BLOCK = 64
S5_CHUNK = 4
S5_PITCH = S5_BLOCK + 8
MLP_ROWS = 1024
MLP_FF_CHUNK = 1024
MLP_DOWN_ROWS = 256


def _rms_scale(x):
    return x * lax.rsqrt(jnp.mean(x * x, axis=-1, keepdims=True) + EPS)


def _rmsnorm(x, w):
    return _rms_scale(x) * w


def _bf16_parts(x, n):
    parts = []
    for _ in range(n):
        part = x.astype(BF16)
        parts.append(part)
        x = x - part.astype(F32)
    return parts


def _gla_substep(x_ref, p_new_ref, p_ref, st_ref, o_ref, first, wg_ref, wgk_ref, bgk_ref, gnw_ref,
                 tri, causal_heads, key_head, value_head, same_head, span_of_t):
    nb, L, _ = x_ref.shape
    n_spans = L // GLA_SPAN
    seqs = range(nb)
    cols = lambda i, c0, c1: p_ref[i * L:(i + 1) * L, c0:c1]

    hb = _rms_scale(x_ref[...].reshape(nb * L, D_MODEL)).astype(BF16)
    n_cols = wg_ref.shape[1]
    tiles = [(wg_ref, c0, c0) for c0 in range(0, n_cols, MXU_DIM)] + [(wgk_ref, 0, n_cols)]

    def project_tile():
        if tiles:
            w_ref, c0, dst = tiles.pop(0)
            p_new_ref[:, dst:dst + MXU_DIM] = jnp.dot(hb, w_ref[:, c0:c0 + MXU_DIM], preferred_element_type=F32)

    log_a = []
    for i in seqs:
        gk = cols(i, n_cols, n_cols + GLA_KEY_DIM) + bgk_ref[...]
        log_a.append((jnp.minimum(gk, 0.0) - jnp.log1p(jnp.exp(-jnp.abs(gk)))) * (1.0 / GLA_GATE_NORMALIZER))
    project_tile()
    project_tile()

    b = []
    for i in seqs:
        terms = jnp.dot(tri, jnp.concatenate(_bf16_parts(log_a[i], 3), axis=1), preferred_element_type=F32)
        b.append(sum(terms[:, j * GLA_KEY_DIM:(j + 1) * GLA_KEY_DIM] for j in range(3)))
    span_row = lambda t, r: jnp.concatenate(
        [jnp.broadcast_to(t[m * GLA_SPAN + r:m * GLA_SPAN + r + 1, :], (GLA_SPAN, GLA_KEY_DIM))
         for m in range(n_spans)], axis=0)
    b_mid = [span_row(b[i], GLA_CHUNK - 1) for i in seqs]
    b_last = [span_row(b[i], GLA_SPAN - 1) for i in seqs]
    project_tile()
    project_tile()

    q = [cols(i, 0, 256) * (GLA_DK ** -0.5) for i in seqs]
    q_s = [q[i] * jnp.exp(b[i] - b_mid[i]) for i in seqs]
    q_b = [(q[i] * jnp.exp(b[i])).astype(BF16) for i in seqs]
    k_et = [(cols(i, 256, 512) * jnp.exp(b_mid[i] - b[i])).T.astype(BF16) for i in seqs]
    k_dt = [(cols(i, 256, 512) * jnp.exp(b_last[i] - b[i])).T.astype(BF16) for i in seqs]
    b_t = [b[i].T for i in seqs]
    v_b = [cols(i, 512, 1024).astype(BF16) for i in seqs]
    project_tile()

    pair = MXU_DIM // GLA_DV
    q_sb = [q_s[i].astype(BF16) for i in seqs]
    o_intra = [[] for _ in seqs]
    for m in range(n_spans):
        r0, r1 = m * GLA_SPAN, (m + 1) * GLA_SPAN
        for i in seqs:
            k_span = k_et[i][:, r0:r1]
            k_blk = jnp.concatenate([jnp.where(key_head == hh, k_span, jnp.zeros_like(k_span))
                                     for hh in range(GLA_HEADS)], axis=1)
            sc = jnp.dot(q_sb[i][r0:r1], k_blk, preferred_element_type=F32)
            sc = jnp.where(causal_heads, sc, 0.0).astype(BF16)
            outs = []
            for g in range(GLA_HEADS // pair):
                v_pair = v_b[i][r0:r1, g * MXU_DIM:(g + 1) * MXU_DIM]
                v_blk = jnp.concatenate([jnp.where(value_head == j, v_pair, jnp.zeros_like(v_pair))
                                         for j in range(pair)], axis=0)
                outs.append(jnp.dot(sc[:, g * pair * GLA_SPAN:(g + 1) * pair * GLA_SPAN], v_blk,
                                    preferred_element_type=F32))
            o_intra[i].append(jnp.concatenate(outs, axis=1))
        project_tile()
        project_tile()
        project_tile()

    st = [jnp.where(first, 0.0, st_ref[i]) for i in seqs]
    o_inter = [[] for _ in seqs]
    for m in range(n_spans):
        r0, r1 = m * GLA_SPAN, (m + 1) * GLA_SPAN
        for i in seqs:
            o_inter[i].append(jnp.dot(q_b[i][r0:r1], st[i].astype(BF16), preferred_element_type=F32))
            k_m = jnp.where(span_of_t == m, k_dt[i], jnp.zeros_like(k_dt[i]))
            upd = jnp.dot(k_m, v_b[i], preferred_element_type=F32)
            decay = jnp.exp(b_t[i][:, r1 - 1:r1])
            st[i] = jnp.where(same_head, st[i] * decay + upd, 0.0)
        project_tile()
        project_tile()

    for i in seqs:
        st_ref[i] = st[i]
        o = jnp.concatenate(o_intra[i], axis=0) + jnp.concatenate(o_inter[i], axis=0)
        outs = []
        for hh in range(GLA_HEADS):
            oh = o[:, hh * GLA_DV:(hh + 1) * GLA_DV]
            outs.append(oh * lax.rsqrt(jnp.mean(oh * oh, axis=-1, keepdims=True) + EPS) * gnw_ref[...])
        g = cols(i, 1024, 1536)
        o_ref[i] = (jnp.concatenate(outs, axis=1) * (g * (1.0 / (1.0 + jnp.exp(-g))))).astype(o_ref.dtype)
    while tiles:
        project_tile()


def _gla_kernel(blocks_per_seq, xa_ref, xb_ref, wg_ref, wgl_ref, wup_ref, bgk_ref, gnw_ref, o_ref,
                st_ref, pa_ref, pb_ref, wgk_ref):
    s = pl.program_id(0)
    L = xa_ref.shape[1]

    @pl.when(s == 0)
    def _():
        st_ref[...] = jnp.zeros_like(st_ref)
        pb_ref[...] = jnp.zeros_like(pb_ref)
        wgk_ref[...] = jnp.dot(wgl_ref[...], wup_ref[...], precision=lax.Precision.HIGHEST,
                               preferred_element_type=F32).astype(BF16)

    row = lax.broadcasted_iota(jnp.int32, (L, L), 0)
    col = lax.broadcasted_iota(jnp.int32, (L, L), 1)
    causal = ((row // GLA_SPAN) == (col // GLA_SPAN)) & (col <= row)
    tri = jnp.where(causal, 1.0, 0.0).astype(BF16)
    srow = lax.broadcasted_iota(jnp.int32, st_ref.shape[1:], 0) // GLA_DK
    scol = lax.broadcasted_iota(jnp.int32, st_ref.shape[1:], 1) // GLA_DV
    same_head = srow == scol
    span_of_t = lax.broadcasted_iota(jnp.int32, (GLA_KEY_DIM, L), 1) // GLA_SPAN
    srow = lax.broadcasted_iota(jnp.int32, (GLA_SPAN, GLA_HEADS * GLA_SPAN), 0)
    scol = lax.broadcasted_iota(jnp.int32, (GLA_SPAN, GLA_HEADS * GLA_SPAN), 1) % GLA_SPAN
    causal_heads = scol <= srow
    key_head = lax.broadcasted_iota(jnp.int32, (GLA_KEY_DIM, GLA_SPAN), 0) // GLA_DK
    value_head = lax.broadcasted_iota(jnp.int32, (GLA_SPAN, MXU_DIM), 1) // GLA_DV
    shared = (wg_ref, wgk_ref, bgk_ref, gnw_ref, tri, causal_heads, key_head, value_head, same_head, span_of_t)

    _gla_substep(xa_ref, pa_ref, pb_ref, st_ref, o_ref.at[0], (2 * s - 1) % blocks_per_seq == 0, *shared)
    _gla_substep(xb_ref, pb_ref, pa_ref, st_ref, o_ref.at[1], (2 * s) % blocks_per_seq == 0, *shared)


def _gla_call(x, wg, wgl, wup, bgk, gnw):
    bsz, seq, _ = x.shape
    nb, L = GLA_SEQS, GLA_BLOCK
    bps = seq // L
    n_blocks = (bsz // nb) * bps
    assert n_blocks % 2 == 0 and bps % 2 == 0
    const = lambda shape: pl.BlockSpec(shape, lambda s: (0,) * len(shape))

    def x_spec(e):
        def index(s):
            n = jnp.minimum(2 * s + e, n_blocks - 1)
            return (n // bps, n % bps, 0)
        return pl.BlockSpec((nb, L, D_MODEL), index)

    return pl.pallas_call(
        functools.partial(_gla_kernel, bps),
        grid=(n_blocks // 2 + 1,),
        in_specs=[x_spec(0), x_spec(1), const(wg.shape), const(wgl.shape), const(wup.shape), const(bgk.shape),
                  const(gnw.shape)],
        out_specs=pl.BlockSpec((2, nb, L, GLA_WIDTH), lambda s: (s, 0, 0, 0)),
        out_shape=jax.ShapeDtypeStruct((n_blocks + 2, nb, L, GLA_WIDTH), BF16),
        scratch_shapes=[pltpu.VMEM((nb, GLA_HEADS * GLA_DK, GLA_HEADS * GLA_DV), F32),
                        pltpu.VMEM((nb * L, wg.shape[1] + GLA_KEY_DIM), F32),
                        pltpu.VMEM((nb * L, wg.shape[1] + GLA_KEY_DIM), F32),
                        pltpu.VMEM((D_MODEL, GLA_KEY_DIM), BF16)],
        compiler_params=pltpu.CompilerParams(
            dimension_semantics=("arbitrary",), vmem_limit_bytes=VMEM_LIMIT_BYTES),
        name="gla_mixer",
    )(x, x, wg, wgl, wup, bgk, gnw)


def _gelu_tanh(y):
    return 0.5 * y * (1.0 + jnp.tanh(np.float32(math.sqrt(2.0 / math.pi)) * (y + 0.044715 * (y * y * y))))


def _s5_kernel(x_ref, wu_ref, win_ref, wintra_ref, wout_ref, ajr_ref, aji_ref, wglu_ref, bglu_ref,
               o_ref, u_ref, st_ref, carry_ref, y_ref, oscr_ref):
    bsz, L, _ = x_ref.shape
    J, P = S5_CHUNK, S5_PITCH
    n_c = L // J
    n_k = S5_WIDTH // LANES
    half = LANES // 2
    hs_states = 2 * S5_STATES // (2 * n_k)
    q_pairs = hs_states // 2 // LANES
    assert J * half == MXU_DIM and J % 2 == 0
    low = lax.broadcasted_iota(jnp.int32, (1, LANES), 1) < half
    swap = lambda t: pltpu.roll(t, half, axis=1)

    @pl.when(pl.program_id(0) == 0)
    def _():
        carry_ref[...] = jnp.zeros_like(carry_ref)

    h = _rms_scale(x_ref[...].reshape(bsz * L, D_MODEL))
    u = jnp.dot(h.astype(BF16), wu_ref[...], preferred_element_type=F32)
    for bb in range(bsz):
        for k in range(n_k):
            u_ref[k, bb * P:bb * P + L, :] = u[bb * L:(bb + 1) * L, k * LANES:(k + 1) * LANES]

    for k in range(n_k):
        lhs = [[], []]
        for c in range(n_c):
            pieces = [u_ref[k, pl.ds(c * J + i, bsz, stride=P), :] for i in range(J)]
            turned = [swap(p) for p in pieces]
            lhs[0].append(jnp.concatenate(
                [jnp.where(low, pieces[i], turned[i + 1]) for i in range(0, J, 2)], axis=1))
            lhs[1].append(jnp.concatenate(
                [jnp.where(low, turned[i], pieces[i + 1]) for i in range(0, J, 2)], axis=1))

        for hh in range(2):
            hs = 2 * k + hh
            s0 = hs * hs_states
            utb = jnp.concatenate(lhs[hh], axis=0).astype(BF16)
            st_ref[:, s0:s0 + hs_states] = jnp.dot(utb, win_ref[hs], preferred_element_type=F32)
            y_ref[hs] = jnp.dot(utb, wintra_ref[hs], preferred_element_type=F32)

            for q in range(q_pairs):
                lr = s0 + q * LANES
                li = lr + hs_states // 2
                ar = jnp.broadcast_to(ajr_ref[hs:hs + 1, q * LANES:(q + 1) * LANES], (bsz, LANES))
                ai = jnp.broadcast_to(aji_ref[hs:hs + 1, q * LANES:(q + 1) * LANES], (bsz, LANES))
                xr = carry_ref[:, lr:lr + LANES]
                xi = carry_ref[:, li:li + LANES]
                for c in range(n_c):
                    rows = slice(c * bsz, (c + 1) * bsz)
                    sr = st_ref[rows, lr:lr + LANES]
                    si = st_ref[rows, li:li + LANES]
                    st_ref[rows, lr:lr + LANES] = xr
                    st_ref[rows, li:li + LANES] = xi
                    xr, xi = ar * xr - ai * xi + sr, ar * xi + ai * xr + si
                carry_ref[:, lr:lr + LANES] = xr
                carry_ref[:, li:li + LANES] = xi

            xs = st_ref[:, s0:s0 + hs_states].astype(BF16)
            y_ref[hs] = _gelu_tanh(y_ref[hs] + jnp.dot(xs, wout_ref[hs], preferred_element_type=F32))

    for r in range(J):
        v = (r // 2) * LANES
        slabs = []
        for k in range(n_k):
            ya, yb = y_ref[2 * k, :, v:v + LANES], y_ref[2 * k + 1, :, v:v + LANES]
            slabs.append(jnp.where(low, ya, swap(yb)) if r % 2 == 0 else jnp.where(low, swap(ya), yb))
        z = jnp.concatenate(slabs, axis=1)
        gate = jnp.dot(z.astype(BF16), wglu_ref[...], preferred_element_type=F32) + bglu_ref[...]
        o_r = z * (1.0 / (1.0 + jnp.exp(-gate)))
        for c in range(n_c):
            for k in range(n_k):
                oscr_ref[k, pl.ds(c * J + r, bsz, stride=P), :] = o_r[c * bsz:(c + 1) * bsz, k * LANES:(k + 1) * LANES]
    for bb in range(bsz):
        for k in range(n_k):
            o_ref[bb, :, k * LANES:(k + 1) * LANES] = oscr_ref[k, bb * P:bb * P + L, :].astype(o_ref.dtype)


def _s5_call(x, wu, win, wintra, wout, ajr, aji, wglu, bglu):
    bsz, seq, _ = x.shape
    L, J = S5_BLOCK, S5_CHUNK
    rows_c = (L // J) * bsz
    n_k = S5_WIDTH // LANES
    const = lambda shape: pl.BlockSpec(shape, lambda t: (0,) * len(shape))
    return pl.pallas_call(
        _s5_kernel,
        grid=(seq // L,),
        in_specs=[pl.BlockSpec((bsz, L, D_MODEL), lambda t: (0, t, 0))]
        + [const(a.shape) for a in (wu, win, wintra, wout, ajr, aji, wglu, bglu)],
        out_specs=pl.BlockSpec((bsz, L, S5_WIDTH), lambda t: (0, t, 0)),
        out_shape=jax.ShapeDtypeStruct((bsz, seq, S5_WIDTH), BF16),
        scratch_shapes=[
            pltpu.VMEM((n_k, bsz * S5_PITCH, LANES), F32),
            pltpu.VMEM((rows_c, 2 * S5_STATES), F32),
            pltpu.VMEM((bsz, 2 * S5_STATES), F32),
            pltpu.VMEM((2 * n_k, rows_c, J * LANES // 2), F32),
            pltpu.VMEM((n_k, bsz * S5_PITCH, LANES), F32),
        ],
        compiler_params=pltpu.CompilerParams(
            dimension_semantics=("arbitrary",), vmem_limit_bytes=VMEM_LIMIT_BYTES),
        name="s5_mixer",
    )(x, wu, win, wintra, wout, ajr, aji, wglu, bglu)


def _s5_prepare(a_re, a_im, log_dt, b_re, b_im, c_re, c_im, d_skip):
    J = S5_CHUNK
    width = LANES // 2
    n_k = S5_WIDTH // width
    gpk = S5_GROUPS // n_k
    lanes = gpk * S5_STATE
    hi = lax.Precision.HIGHEST

    per_state = lambda t: t.reshape(n_k, 1, lanes)
    ar, ai = per_state(a_re), per_state(a_im)
    dt = per_state(jnp.broadcast_to(jnp.exp(log_dt)[:, None], a_re.shape))
    b_t = lambda t: t.reshape(n_k, gpk, S5_STATE, S5_GROUP).transpose(0, 3, 1, 2).reshape(n_k, S5_GROUP, lanes)
    c_t = lambda t: t.reshape(n_k, gpk, S5_GROUP, S5_STATE).transpose(0, 2, 1, 3).reshape(n_k, S5_GROUP, lanes)
    b_re, b_im, c_re, c_im = b_t(b_re), b_t(b_im), c_t(c_re), c_t(c_im)

    def apow(m):
        mag = jnp.exp(m * ar * dt)
        return mag * jnp.cos(m * ai * dt), mag * jnp.sin(m * ai * dt)

    abr, abi = apow(1)
    den = ar * ar + ai * ai
    nr = abr - 1.0
    fr = (nr * ar + abi * ai) / den
    fi = (abi * ar - nr * ai) / den
    bbr = fr * b_re - fi * b_im
    bbi = fr * b_im + fi * b_re

    ab_re, ab_im = [], []
    for m in range(J):
        pr, pi = apow(m)
        ab_re.append(pr * bbr - pi * bbi)
        ab_im.append(pr * bbi + pi * bbr)

    def block_diag(t, lane_period, lane_group_width):
        n_lanes = t.shape[-1]
        row_g = lax.broadcasted_iota(jnp.int32, (gpk, 1, n_lanes), 0)
        lane_g = (lax.broadcasted_iota(jnp.int32, (gpk, 1, n_lanes), 2) % lane_period) // lane_group_width
        full = jnp.where(row_g == lane_g, t[:, :, None, :, :], 0.0)
        return full.reshape(n_k, J * width, n_lanes)

    win = block_diag(jnp.stack(
        [jnp.concatenate([ab_re[J - 1 - i], ab_im[J - 1 - i]], axis=-1) for i in range(J)], axis=1),
        lanes, S5_STATE)

    t_out = []
    for r in range(J):
        pr, pi = apow(r + 1)
        t_out.append(jnp.concatenate([c_re * pr - c_im * pi, -(c_re * pi + c_im * pr)], axis=-1))
    wout = block_diag(jnp.stack(t_out, axis=1), lanes, S5_STATE).transpose(0, 2, 1)

    grp = lambda t: t.reshape(n_k, S5_GROUP, gpk, S5_STATE)
    kmat = [(jnp.einsum('kqgn,kpgn->kpgq', grp(c_re), grp(ab_re[m]), precision=hi)
             - jnp.einsum('kqgn,kpgn->kpgq', grp(c_im), grp(ab_im[m]), precision=hi)).reshape(n_k, S5_GROUP, width)
            for m in range(J)]
    d_t = d_skip.reshape(n_k, gpk, S5_GROUP).transpose(0, 2, 1)
    p_eq_q = (lax.broadcasted_iota(jnp.int32, (S5_GROUP, gpk, S5_GROUP), 0)
              == lax.broadcasted_iota(jnp.int32, (S5_GROUP, gpk, S5_GROUP), 2))
    kmat[0] = kmat[0] + jnp.where(p_eq_q, d_t[..., None], 0.0).reshape(n_k, S5_GROUP, width)
    zero = jnp.zeros_like(kmat[0])
    t_intra = jnp.stack([jnp.concatenate([kmat[r - i] if i <= r else zero for r in range(J)], axis=-1)
                         for i in range(J)], axis=1)
    wintra = block_diag(t_intra, width, S5_GROUP)

    ajr, aji = apow(J)
    return (win.astype(BF16), wintra.astype(BF16), wout.astype(BF16), ajr.reshape(n_k, lanes), aji.reshape(n_k, lanes))


def _out_mlp_kernel(n_og, x_ref, *refs):
    og_refs = refs[:n_og]
    os_ref, wo_ref, wup_ref, wdn_ref, fw_ref, o_ref = refs[n_og:]
    L = og_refs[0].shape[0]
    x1_groups = [x_ref[j * L:(j + 1) * L, :]
                 + jnp.dot(og_ref[...], wo_ref[0:GLA_WIDTH, :], preferred_element_type=F32)
                 + jnp.dot(os_ref[j * L:(j + 1) * L, :], wo_ref[GLA_WIDTH:, :], preferred_element_type=F32)
                 for j, og_ref in enumerate(og_refs)]
    h = jnp.concatenate([_rms_scale(g).astype(BF16) for g in x1_groups], axis=0)
    x1 = jnp.concatenate(x1_groups, axis=0)
    acts = []
    for c in range(D_FF // MLP_FF_CHUNK):
        cs = slice(c * MLP_FF_CHUNK, (c + 1) * MLP_FF_CHUNK)
        a = jnp.maximum(jnp.dot(h, wup_ref[:, cs], preferred_element_type=F32), 0.0)
        acts.append((a * a).astype(BF16))
    act = jnp.concatenate(acts, axis=1)
    for r0 in range(0, MLP_ROWS, MLP_DOWN_ROWS):
        r = slice(r0, r0 + MLP_DOWN_ROWS)
        x2 = x1[r] + jnp.dot(act[r], wdn_ref[...], preferred_element_type=F32)
        o_ref[r, :] = _rmsnorm(x2, fw_ref[...])


def _out_mlp_call(x2, og_blocks, os_, wo, wup, wdn, fw, seq):
    rows = x2.shape[0]
    nb, L = og_blocks.shape[1], og_blocks.shape[2]
    assert MLP_ROWS % L == 0 and seq % MLP_ROWS == 0
    n_og = MLP_ROWS // L
    tiles_per_seq = seq // MLP_ROWS
    const = lambda shape: pl.BlockSpec(shape, lambda i: (0,) * len(shape), pipeline_mode=pl.Buffered(1))
    tile = lambda width: pl.BlockSpec((MLP_ROWS, width), lambda i: (i, 0))

    def og_spec(e):
        def index(i):
            b, t = i // tiles_per_seq, i % tiles_per_seq
            return ((b // nb) * (seq // L) + n_og * t + e + 1, b % nb, 0, 0)
        return pl.BlockSpec((None, None, L, GLA_WIDTH), index)

    return pl.pallas_call(
        functools.partial(_out_mlp_kernel, n_og),
        grid=(rows // MLP_ROWS,),
        in_specs=[tile(D_MODEL)] + [og_spec(e) for e in range(n_og)] + [tile(S5_WIDTH)]
        + [const(a.shape) for a in (wo, wup, wdn, fw)],
        out_specs=tile(D_MODEL),
        out_shape=jax.ShapeDtypeStruct((rows, D_MODEL), F32),
        compiler_params=pltpu.CompilerParams(
            dimension_semantics=("arbitrary",), vmem_limit_bytes=VMEM_LIMIT_BYTES),
        name="out_mlp",
    )(x2, *([og_blocks] * n_og), os_, wo, wup, wdn, fw)


def kernel(x, norm_mix_w, w_in, w_gk_up, b_gk, gla_norm_w, s5_a_re, s5_a_im, s5_log_dt, s5_b_re, s5_b_im,
           s5_c_re, s5_c_im, s5_d, w_glu, b_glu, w_out, norm_mlp_w, w_mlp_up, w_mlp_down, norm_final_w):
    bsz, seq, _ = x.shape
    assert bsz % GLA_SEQS == 0 and seq % GLA_BLOCK == 0 and seq % S5_BLOCK == 0 and (bsz * seq) % MLP_ROWS == 0
    assert norm_mix_w.shape[0] == 1, "single-layer problem"
    for l in range(1):
        w = w_in[l] * norm_mix_w[l][:, None]
        kd2, r = 2 * GLA_KEY_DIM, GLA_GATE_RANK
        wg = jnp.concatenate([w[:, 0:kd2 + GLA_WIDTH], w[:, kd2 + GLA_WIDTH + r:kd2 + 2 * GLA_WIDTH + r]],
                             axis=1).astype(BF16)
        wgl = jnp.concatenate([w[:, kd2 + GLA_WIDTH:kd2 + GLA_WIDTH + r], jnp.zeros((D_MODEL, LANES - r), F32)], axis=1)
        wu = w[:, kd2 + 2 * GLA_WIDTH + r:].astype(BF16)
        wup = jnp.concatenate([w_gk_up[l], jnp.zeros((LANES - r, GLA_KEY_DIM), F32)], axis=0)

        o_gla = _gla_call(x, wg, wgl, wup, b_gk[l][None, :], gla_norm_w[l][None, :])

        win, wintra, wout, ajr, aji = _s5_prepare(s5_a_re[l], s5_a_im[l], s5_log_dt[l], s5_b_re[l], s5_b_im[l],
                                                  s5_c_re[l], s5_c_im[l], s5_d[l])
        o_s5 = _s5_call(x, wu, win, wintra, wout, ajr, aji, w_glu[l].astype(BF16), b_glu[l][None, :])

        rows = bsz * seq
        out = _out_mlp_call(x.reshape(rows, D_MODEL), o_gla, o_s5.reshape(rows, S5_WIDTH),
                            w_out[l].astype(BF16), (w_mlp_up[l] * norm_mlp_w[l][:, None]).astype(BF16),
                            w_mlp_down[l].astype(BF16), norm_final_w[None, :], seq)
        x = out.reshape(bsz, seq, D_MODEL)
    return x
```

```python
import functools
import math

import jax
import jax.numpy as jnp
import numpy as np
from jax import lax
from jax.experimental import pallas as pl
from jax.experimental.pallas import tpu as pltpu

F32 = jnp.float32
BF16 = jnp.bfloat16

D_MODEL = 1024
GLA_WIDTH = 512
S5_WIDTH = 512
GLA_HEADS = 4
GLA_KEY_DIM = 256
GLA_DK = 64
GLA_DV = 128
GLA_GATE_RANK = 16
GLA_GATE_NORMALIZER = 16.0
GLA_CHUNK = 64
GLA_SPAN = 2 * GLA_CHUNK
S5_GROUP = 16
S5_GROUPS = 32
S5_STATE = 64
D_FF = 4096
EPS = 1e-6

LANES = 128
MXU_DIM = 256
S5_STATES = S5_GROUPS * S5_STATE
VMEM_LIMIT_BYTES = 56 * 1024 * 1024

GLA_BLOCK = 256
GLA_SEQS = 2
S5_BLOCK = 64
S5_CHUNK = 4
S5_PITCH = S5_BLOCK + 8
MLP_ROWS = 1024
MLP_FF_CHUNK = 1024
MLP_DOWN_ROWS = 256


def _rms_scale(x):
    return x * lax.rsqrt(jnp.mean(x * x, axis=-1, keepdims=True) + EPS)


def _rmsnorm(x, w):
    return _rms_scale(x) * w


def _bf16_parts(x, n):
    parts = []
    for _ in range(n):
        part = x.astype(BF16)
        parts.append(part)
        x = x - part.astype(F32)
    return parts


def _gla_substep(x_ref, p_new_ref, p_ref, st_ref, o_ref, first, wg_ref, wgk_ref, bgk_ref, gnw_ref,
                 tri, causal_heads, key_head, value_head, same_head, span_of_t):
    nb, L, _ = x_ref.shape
    n_spans = L // GLA_SPAN
    seqs = range(nb)
    cols = lambda i, c0, c1: p_ref[i * L:(i + 1) * L, c0:c1]

    hb = _rms_scale(x_ref[...].reshape(nb * L, D_MODEL)).astype(BF16)
    n_cols = wg_ref.shape[1]
    tiles = [(wg_ref, c0, c0) for c0 in range(0, n_cols, MXU_DIM)] + [(wgk_ref, 0, n_cols)]

    def project_tile():
        if tiles:
            w_ref, c0, dst = tiles.pop(0)
            p_new_ref[:, dst:dst + MXU_DIM] = jnp.dot(hb, w_ref[:, c0:c0 + MXU_DIM], preferred_element_type=F32)

    log_a = []
    for i in seqs:
        gk = cols(i, n_cols, n_cols + GLA_KEY_DIM) + bgk_ref[...]
        log_a.append((jnp.minimum(gk, 0.0) - jnp.log1p(jnp.exp(-jnp.abs(gk)))) * (1.0 / GLA_GATE_NORMALIZER))
    project_tile()
    project_tile()

    b = []
    for i in seqs:
        terms = jnp.dot(tri, jnp.concatenate(_bf16_parts(log_a[i], 3), axis=1), preferred_element_type=F32)
        b.append(sum(terms[:, j * GLA_KEY_DIM:(j + 1) * GLA_KEY_DIM] for j in range(3)))
    span_row = lambda t, r: jnp.concatenate(
        [jnp.broadcast_to(t[m * GLA_SPAN + r:m * GLA_SPAN + r + 1, :], (GLA_SPAN, GLA_KEY_DIM))
         for m in range(n_spans)], axis=0)
    b_mid = [span_row(b[i], GLA_CHUNK - 1) for i in seqs]
    b_last = [span_row(b[i], GLA_SPAN - 1) for i in seqs]
    project_tile()
    project_tile()

    q = [cols(i, 0, 256) * (GLA_DK ** -0.5) for i in seqs]
    q_s = [q[i] * jnp.exp(b[i] - b_mid[i]) for i in seqs]
    q_b = [(q[i] * jnp.exp(b[i])).astype(BF16) for i in seqs]
    k_et = [(cols(i, 256, 512) * jnp.exp(b_mid[i] - b[i])).T.astype(BF16) for i in seqs]
    k_dt = [(cols(i, 256, 512) * jnp.exp(b_last[i] - b[i])).T.astype(BF16) for i in seqs]
    b_t = [b[i].T for i in seqs]
    v_b = [cols(i, 512, 1024).astype(BF16) for i in seqs]
    project_tile()

    pair = MXU_DIM // GLA_DV
    q_sb = [q_s[i].astype(BF16) for i in seqs]
    o_intra = [[] for _ in seqs]
    for m in range(n_spans):
        r0, r1 = m * GLA_SPAN, (m + 1) * GLA_SPAN
        for i in seqs:
            k_span = k_et[i][:, r0:r1]
            k_blk = jnp.concatenate([jnp.where(key_head == hh, k_span, jnp.zeros_like(k_span))
                                     for hh in range(GLA_HEADS)], axis=1)
            sc = jnp.dot(q_sb[i][r0:r1], k_blk, preferred_element_type=F32)
            sc = jnp.where(causal_heads, sc, 0.0).astype(BF16)
            outs = []
            for g in range(GLA_HEADS // pair):
                v_pair = v_b[i][r0:r1, g * MXU_DIM:(g + 1) * MXU_DIM]
                v_blk = jnp.concatenate([jnp.where(value_head == j, v_pair, jnp.zeros_like(v_pair))
                                         for j in range(pair)], axis=0)
                outs.append(jnp.dot(sc[:, g * pair * GLA_SPAN:(g + 1) * pair * GLA_SPAN], v_blk,
                                    preferred_element_type=F32))
            o_intra[i].append(jnp.concatenate(outs, axis=1))
        project_tile()
        project_tile()
        project_tile()

    st = [jnp.where(first, 0.0, st_ref[i]) for i in seqs]
    o_inter = [[] for _ in seqs]
    for m in range(n_spans):
        r0, r1 = m * GLA_SPAN, (m + 1) * GLA_SPAN
        for i in seqs:
            o_inter[i].append(jnp.dot(q_b[i][r0:r1], st[i].astype(BF16), preferred_element_type=F32))
            k_m = jnp.where(span_of_t == m, k_dt[i], jnp.zeros_like(k_dt[i]))
            upd = jnp.dot(k_m, v_b[i], preferred_element_type=F32)
            decay = jnp.exp(b_t[i][:, r1 - 1:r1])
            st[i] = jnp.where(same_head, st[i] * decay + upd, 0.0)
        project_tile()
        project_tile()

    for i in seqs:
        st_ref[i] = st[i]
        o = jnp.concatenate(o_intra[i], axis=0) + jnp.concatenate(o_inter[i], axis=0)
        outs = []
        for hh in range(GLA_HEADS):
            oh = o[:, hh * GLA_DV:(hh + 1) * GLA_DV]
            outs.append(oh * lax.rsqrt(jnp.mean(oh * oh, axis=-1, keepdims=True) + EPS) * gnw_ref[...])
        g = cols(i, 1024, 1536)
        o_ref[i] = (jnp.concatenate(outs, axis=1) * (g * (1.0 / (1.0 + jnp.exp(-g))))).astype(o_ref.dtype)
    while tiles:
        project_tile()


def _gla_kernel(blocks_per_seq, xa_ref, xb_ref, wg_ref, wgl_ref, wup_ref, bgk_ref, gnw_ref, o_ref,
                st_ref, pa_ref, pb_ref, wgk_ref):
    s = pl.program_id(0)
    L = xa_ref.shape[1]

    @pl.when(s == 0)
    def _():
        st_ref[...] = jnp.zeros_like(st_ref)
        pb_ref[...] = jnp.zeros_like(pb_ref)
        wgk_ref[...] = jnp.dot(wgl_ref[...], wup_ref[...], precision=lax.Precision.HIGHEST,
                               preferred_element_type=F32).astype(BF16)

    row = lax.broadcasted_iota(jnp.int32, (L, L), 0)
    col = lax.broadcasted_iota(jnp.int32, (L, L), 1)
    causal = ((row // GLA_SPAN) == (col // GLA_SPAN)) & (col <= row)
    tri = jnp.where(causal, 1.0, 0.0).astype(BF16)
    srow = lax.broadcasted_iota(jnp.int32, st_ref.shape[1:], 0) // GLA_DK
    scol = lax.broadcasted_iota(jnp.int32, st_ref.shape[1:], 1) // GLA_DV
    same_head = srow == scol
    span_of_t = lax.broadcasted_iota(jnp.int32, (GLA_KEY_DIM, L), 1) // GLA_SPAN
    srow = lax.broadcasted_iota(jnp.int32, (GLA_SPAN, GLA_HEADS * GLA_SPAN), 0)
    scol = lax.broadcasted_iota(jnp.int32, (GLA_SPAN, GLA_HEADS * GLA_SPAN), 1) % GLA_SPAN
    causal_heads = scol <= srow
    key_head = lax.broadcasted_iota(jnp.int32, (GLA_KEY_DIM, GLA_SPAN), 0) // GLA_DK
    value_head = lax.broadcasted_iota(jnp.int32, (GLA_SPAN, MXU_DIM), 1) // GLA_DV
    shared = (wg_ref, wgk_ref, bgk_ref, gnw_ref, tri, causal_heads, key_head, value_head, same_head, span_of_t)

    _gla_substep(xa_ref, pa_ref, pb_ref, st_ref, o_ref.at[0], (2 * s - 1) % blocks_per_seq == 0, *shared)
    _gla_substep(xb_ref, pb_ref, pa_ref, st_ref, o_ref.at[1], (2 * s) % blocks_per_seq == 0, *shared)


def _gla_call(x, wg, wgl, wup, bgk, gnw):
    bsz, seq, _ = x.shape
    nb, L = GLA_SEQS, GLA_BLOCK
    bps = seq // L
    n_blocks = (bsz // nb) * bps
    assert n_blocks % 2 == 0 and bps % 2 == 0
    const = lambda shape: pl.BlockSpec(shape, lambda s: (0,) * len(shape))

    def x_spec(e):
        def index(s):
            n = jnp.minimum(2 * s + e, n_blocks - 1)
            return (n // bps, n % bps, 0)
        return pl.BlockSpec((nb, L, D_MODEL), index)

    return pl.pallas_call(
        functools.partial(_gla_kernel, bps),
        grid=(n_blocks // 2 + 1,),
        in_specs=[x_spec(0), x_spec(1), const(wg.shape), const(wgl.shape), const(wup.shape), const(bgk.shape),
                  const(gnw.shape)],
        out_specs=pl.BlockSpec((2, nb, L, GLA_WIDTH), lambda s: (s, 0, 0, 0)),
        out_shape=jax.ShapeDtypeStruct((n_blocks + 2, nb, L, GLA_WIDTH), BF16),
        scratch_shapes=[pltpu.VMEM((nb, GLA_HEADS * GLA_DK, GLA_HEADS * GLA_DV), F32),
                        pltpu.VMEM((nb * L, wg.shape[1] + GLA_KEY_DIM), F32),
                        pltpu.VMEM((nb * L, wg.shape[1] + GLA_KEY_DIM), F32),
                        pltpu.VMEM((D_MODEL, GLA_KEY_DIM), BF16)],
        compiler_params=pltpu.CompilerParams(
            dimension_semantics=("arbitrary",), vmem_limit_bytes=VMEM_LIMIT_BYTES),
        name="gla_mixer",
    )(x, x, wg, wgl, wup, bgk, gnw)


def _gelu_tanh(y):
    return 0.5 * y * (1.0 + jnp.tanh(np.float32(math.sqrt(2.0 / math.pi)) * (y + 0.044715 * (y * y * y))))


def _s5_kernel(x_ref, wu_ref, win_ref, wintra_ref, wout_ref, ajr_ref, aji_ref, wglu_ref, bglu_ref,
               o_ref, u_ref, st_ref, xs_ref, carry_ref, y_ref, oscr_ref):
    bsz, L, _ = x_ref.shape
    J, P = S5_CHUNK, S5_PITCH
    n_c = L // J
    n_k = S5_WIDTH // LANES
    half = LANES // 2
    hs_states = 2 * S5_STATES // (2 * n_k)
    q_pairs = hs_states // 2 // LANES
    assert J * half == MXU_DIM and J % 2 == 0
    low = lax.broadcasted_iota(jnp.int32, (1, LANES), 1) < half
    swap = lambda t: pltpu.roll(t, half, axis=1)

    @pl.when(pl.program_id(0) == 0)
    def _():
        carry_ref[...] = jnp.zeros_like(carry_ref)

    h = _rms_scale(x_ref[...].reshape(bsz * L, D_MODEL))
    u = jnp.dot(h.astype(BF16), wu_ref[...], preferred_element_type=F32)
    for bb in range(bsz):
        for k in range(n_k):
            u_ref[k, bb * P:bb * P + L, :] = u[bb * L:(bb + 1) * L, k * LANES:(k + 1) * LANES]

    for k in range(n_k):
        lhs = [[], []]
        for c in range(n_c):
            pieces = [u_ref[k, pl.ds(c * J + i, bsz, stride=P), :] for i in range(J)]
            turned = [swap(p) for p in pieces]
            lhs[0].append(jnp.concatenate(
                [jnp.where(low, pieces[i], turned[i + 1]) for i in range(0, J, 2)], axis=1))
            lhs[1].append(jnp.concatenate(
                [jnp.where(low, turned[i], pieces[i + 1]) for i in range(0, J, 2)], axis=1))

        for hh in range(2):
            hs = 2 * k + hh
            s0 = hs * hs_states
            utb = jnp.concatenate(lhs[hh], axis=0).astype(BF16)
            st_ref[:, s0:s0 + hs_states] = jnp.dot(utb, win_ref[hs], preferred_element_type=F32)
            y_ref[hs] = jnp.dot(utb, wintra_ref[hs], preferred_element_type=F32)

            for q in range(q_pairs):
                lr = s0 + q * LANES
                li = lr + hs_states // 2
                ar = jnp.broadcast_to(ajr_ref[hs:hs + 1, q * LANES:(q + 1) * LANES], (bsz, LANES))
                ai = jnp.broadcast_to(aji_ref[hs:hs + 1, q * LANES:(q + 1) * LANES], (bsz, LANES))
                xr = carry_ref[:, lr:lr + LANES]
                xi = carry_ref[:, li:li + LANES]
                for c in range(n_c):
                    rows = slice(c * bsz, (c + 1) * bsz)
                    xs_ref[rows, lr:lr + LANES] = xr.astype(BF16)
                    xs_ref[rows, li:li + LANES] = xi.astype(BF16)
                    sr = st_ref[rows, lr:lr + LANES]
                    si = st_ref[rows, li:li + LANES]
                    xr, xi = ar * xr - ai * xi + sr, ar * xi + ai * xr + si
                carry_ref[:, lr:lr + LANES] = xr
                carry_ref[:, li:li + LANES] = xi

            xs = xs_ref[:, s0:s0 + hs_states]
            y_ref[hs] = _gelu_tanh(y_ref[hs] + jnp.dot(xs, wout_ref[hs], preferred_element_type=F32))

    for r in range(J):
        v = (r // 2) * LANES
        slabs = []
        for k in range(n_k):
            ya, yb = y_ref[2 * k, :, v:v + LANES], y_ref[2 * k + 1, :, v:v + LANES]
            slabs.append(jnp.where(low, ya, swap(yb)) if r % 2 == 0 else jnp.where(low, swap(ya), yb))
        z = jnp.concatenate(slabs, axis=1)
        gate = jnp.dot(z.astype(BF16), wglu_ref[...], preferred_element_type=F32) + bglu_ref[...]
        o_r = z * (1.0 / (1.0 + jnp.exp(-gate)))
        for c in range(n_c):
            for k in range(n_k):
                oscr_ref[k, pl.ds(c * J + r, bsz, stride=P), :] = o_r[c * bsz:(c + 1) * bsz, k * LANES:(k + 1) * LANES]
    for bb in range(bsz):
        for k in range(n_k):
            o_ref[bb, :, k * LANES:(k + 1) * LANES] = oscr_ref[k, bb * P:bb * P + L, :].astype(o_ref.dtype)


def _s5_call(x, wu, win, wintra, wout, ajr, aji, wglu, bglu):
    bsz, seq, _ = x.shape
    L, J = S5_BLOCK, S5_CHUNK
    rows_c = (L // J) * bsz
    n_k = S5_WIDTH // LANES
    const = lambda shape: pl.BlockSpec(shape, lambda t: (0,) * len(shape))
    return pl.pallas_call(
        _s5_kernel,
        grid=(seq // L,),
        in_specs=[pl.BlockSpec((bsz, L, D_MODEL), lambda t: (0, t, 0))]
        + [const(a.shape) for a in (wu, win, wintra, wout, ajr, aji, wglu, bglu)],
        out_specs=pl.BlockSpec((bsz, L, S5_WIDTH), lambda t: (0, t, 0)),
        out_shape=jax.ShapeDtypeStruct((bsz, seq, S5_WIDTH), BF16),
        scratch_shapes=[
            pltpu.VMEM((n_k, bsz * S5_PITCH, LANES), F32),
            pltpu.VMEM((rows_c, 2 * S5_STATES), F32),
            pltpu.VMEM((rows_c, 2 * S5_STATES), BF16),
            pltpu.VMEM((bsz, 2 * S5_STATES), F32),
            pltpu.VMEM((2 * n_k, rows_c, J * LANES // 2), F32),
            pltpu.VMEM((n_k, bsz * S5_PITCH, LANES), F32),
        ],
        compiler_params=pltpu.CompilerParams(
            dimension_semantics=("arbitrary",), vmem_limit_bytes=VMEM_LIMIT_BYTES),
        name="s5_mixer",
    )(x, wu, win, wintra, wout, ajr, aji, wglu, bglu)


def _s5_prepare(a_re, a_im, log_dt, b_re, b_im, c_re, c_im, d_skip):
    J = S5_CHUNK
    width = LANES // 2
    n_k = S5_WIDTH // width
    gpk = S5_GROUPS // n_k
    lanes = gpk * S5_STATE
    hi = lax.Precision.HIGHEST

    per_state = lambda t: t.reshape(n_k, 1, lanes)
    ar, ai = per_state(a_re), per_state(a_im)
    dt = per_state(jnp.broadcast_to(jnp.exp(log_dt)[:, None], a_re.shape))
    b_t = lambda t: t.reshape(n_k, gpk, S5_STATE, S5_GROUP).transpose(0, 3, 1, 2).reshape(n_k, S5_GROUP, lanes)
    c_t = lambda t: t.reshape(n_k, gpk, S5_GROUP, S5_STATE).transpose(0, 2, 1, 3).reshape(n_k, S5_GROUP, lanes)
    b_re, b_im, c_re, c_im = b_t(b_re), b_t(b_im), c_t(c_re), c_t(c_im)

    def apow(m):
        mag = jnp.exp(m * ar * dt)
        return mag * jnp.cos(m * ai * dt), mag * jnp.sin(m * ai * dt)

    abr, abi = apow(1)
    den = ar * ar + ai * ai
    nr = abr - 1.0
    fr = (nr * ar + abi * ai) / den
    fi = (abi * ar - nr * ai) / den
    bbr = fr * b_re - fi * b_im
    bbi = fr * b_im + fi * b_re

    ab_re, ab_im = [], []
    for m in range(J):
        pr, pi = apow(m)
        ab_re.append(pr * bbr - pi * bbi)
        ab_im.append(pr * bbi + pi * bbr)

    def block_diag(t, lane_period, lane_group_width):
        n_lanes = t.shape[-1]
        row_g = lax.broadcasted_iota(jnp.int32, (gpk, 1, n_lanes), 0)
        lane_g = (lax.broadcasted_iota(jnp.int32, (gpk, 1, n_lanes), 2) % lane_period) // lane_group_width
        full = jnp.where(row_g == lane_g, t[:, :, None, :, :], 0.0)
        return full.reshape(n_k, J * width, n_lanes)

    win = block_diag(jnp.stack(
        [jnp.concatenate([ab_re[J - 1 - i], ab_im[J - 1 - i]], axis=-1) for i in range(J)], axis=1),
        lanes, S5_STATE)

    t_out = []
    for r in range(J):
        pr, pi = apow(r + 1)
        t_out.append(jnp.concatenate([c_re * pr - c_im * pi, -(c_re * pi + c_im * pr)], axis=-1))
    wout = block_diag(jnp.stack(t_out, axis=1), lanes, S5_STATE).transpose(0, 2, 1)

    grp = lambda t: t.reshape(n_k, S5_GROUP, gpk, S5_STATE)
    kmat = [(jnp.einsum('kqgn,kpgn->kpgq', grp(c_re), grp(ab_re[m]), precision=hi)
             - jnp.einsum('kqgn,kpgn->kpgq', grp(c_im), grp(ab_im[m]), precision=hi)).reshape(n_k, S5_GROUP, width)
            for m in range(J)]
    d_t = d_skip.reshape(n_k, gpk, S5_GROUP).transpose(0, 2, 1)
    p_eq_q = (lax.broadcasted_iota(jnp.int32, (S5_GROUP, gpk, S5_GROUP), 0)
              == lax.broadcasted_iota(jnp.int32, (S5_GROUP, gpk, S5_GROUP), 2))
    kmat[0] = kmat[0] + jnp.where(p_eq_q, d_t[..., None], 0.0).reshape(n_k, S5_GROUP, width)
    zero = jnp.zeros_like(kmat[0])
    t_intra = jnp.stack([jnp.concatenate([kmat[r - i] if i <= r else zero for r in range(J)], axis=-1)
                         for i in range(J)], axis=1)
    wintra = block_diag(t_intra, width, S5_GROUP)

    ajr, aji = apow(J)
    return (win.astype(BF16), wintra.astype(BF16), wout.astype(BF16), ajr.reshape(n_k, lanes), aji.reshape(n_k, lanes))


def _out_mlp_kernel(n_og, x_ref, *refs):
    og_refs = refs[:n_og]
    os_ref, wo_ref, wup_ref, wdn_ref, fw_ref, o_ref = refs[n_og:]
    L = og_refs[0].shape[0]
    x1_groups = [x_ref[j * L:(j + 1) * L, :]
                 + jnp.dot(og_ref[...], wo_ref[0:GLA_WIDTH, :], preferred_element_type=F32)
                 + jnp.dot(os_ref[j * L:(j + 1) * L, :], wo_ref[GLA_WIDTH:, :], preferred_element_type=F32)
                 for j, og_ref in enumerate(og_refs)]
    h = jnp.concatenate([_rms_scale(g).astype(BF16) for g in x1_groups], axis=0)
    x1 = jnp.concatenate(x1_groups, axis=0)
    acts = []
    for c in range(D_FF // MLP_FF_CHUNK):
        cs = slice(c * MLP_FF_CHUNK, (c + 1) * MLP_FF_CHUNK)
        a = jnp.maximum(jnp.dot(h, wup_ref[:, cs], preferred_element_type=F32), 0.0)
        acts.append((a * a).astype(BF16))
    act = jnp.concatenate(acts, axis=1)
    for r0 in range(0, MLP_ROWS, MLP_DOWN_ROWS):
        r = slice(r0, r0 + MLP_DOWN_ROWS)
        x2 = x1[r] + jnp.dot(act[r], wdn_ref[...], preferred_element_type=F32)
        o_ref[r, :] = _rmsnorm(x2, fw_ref[...])


def _out_mlp_call(x2, og_blocks, os_, wo, wup, wdn, fw, seq):
    rows = x2.shape[0]
    nb, L = og_blocks.shape[1], og_blocks.shape[2]
    assert MLP_ROWS % L == 0 and seq % MLP_ROWS == 0
    n_og = MLP_ROWS // L
    tiles_per_seq = seq // MLP_ROWS
    const = lambda shape: pl.BlockSpec(shape, lambda i: (0,) * len(shape), pipeline_mode=pl.Buffered(1))
    tile = lambda width: pl.BlockSpec((MLP_ROWS, width), lambda i: (i, 0))

    def og_spec(e):
        def index(i):
            b, t = i // tiles_per_seq, i % tiles_per_seq
            return ((b // nb) * (seq // L) + n_og * t + e + 1, b % nb, 0, 0)
        return pl.BlockSpec((None, None, L, GLA_WIDTH), index)

    return pl.pallas_call(
        functools.partial(_out_mlp_kernel, n_og),
        grid=(rows // MLP_ROWS,),
        in_specs=[tile(D_MODEL)] + [og_spec(e) for e in range(n_og)] + [tile(S5_WIDTH)]
        + [const(a.shape) for a in (wo, wup, wdn, fw)],
        out_specs=tile(D_MODEL),
        out_shape=jax.ShapeDtypeStruct((rows, D_MODEL), F32),
        compiler_params=pltpu.CompilerParams(
            dimension_semantics=("arbitrary",), vmem_limit_bytes=VMEM_LIMIT_BYTES),
        name="out_mlp",
    )(x2, *([og_blocks] * n_og), os_, wo, wup, wdn, fw)


def kernel(x, norm_mix_w, w_in, w_gk_up, b_gk, gla_norm_w, s5_a_re, s5_a_im, s5_log_dt, s5_b_re, s5_b_im,
           s5_c_re, s5_c_im, s5_d, w_glu, b_glu, w_out, norm_mlp_w, w_mlp_up, w_mlp_down, norm_final_w):
    bsz, seq, _ = x.shape
    assert bsz % GLA_SEQS == 0 and seq % GLA_BLOCK == 0 and seq % S5_BLOCK == 0 and (bsz * seq) % MLP_ROWS == 0
    assert norm_mix_w.shape[0] == 1, "single-layer problem"
    for l in range(1):
        gain = norm_mix_w[l][:, None]
        w = (w_in[l] * gain).astype(BF16)
        kd2, r = 2 * GLA_KEY_DIM, GLA_GATE_RANK
        wg = jnp.concatenate([w[:, 0:kd2 + GLA_WIDTH], w[:, kd2 + GLA_WIDTH + r:kd2 + 2 * GLA_WIDTH + r]], axis=1)
        wgl = jnp.concatenate([w_in[l][:, kd2 + GLA_WIDTH:kd2 + GLA_WIDTH + r] * gain,
                               jnp.zeros((D_MODEL, LANES - r), F32)], axis=1)
        wu = w[:, kd2 + 2 * GLA_WIDTH + r:]
        wup = jnp.concatenate([w_gk_up[l], jnp.zeros((LANES - r, GLA_KEY_DIM), F32)], axis=0)

        o_gla = _gla_call(x, wg, wgl, wup, b_gk[l][None, :], gla_norm_w[l][None, :])

        win, wintra, wout, ajr, aji = _s5_prepare(s5_a_re[l], s5_a_im[l], s5_log_dt[l], s5_b_re[l], s5_b_im[l],
                                                  s5_c_re[l], s5_c_im[l], s5_d[l])
        o_s5 = _s5_call(x, wu, win, wintra, wout, ajr, aji, w_glu[l].astype(BF16), b_glu[l][None, :])

        rows = bsz * seq
        out = _out_mlp_call(x.reshape(rows, D_MODEL), o_gla, o_s5.reshape(rows, S5_WIDTH),
                            w_out[l].astype(BF16), (w_mlp_up[l] * norm_mlp_w[l][:, None]).astype(BF16),
                            w_mlp_down[l].astype(BF16), norm_final_w[None, :], seq)
        x = out.reshape(bsz, seq, D_MODEL)
    return x
```

```python
import functools
import math

import jax
import jax.numpy as jnp
import numpy as np
from jax import lax
from jax.experimental import pallas as pl
from jax.experimental.pallas import tpu as pltpu

F32 = jnp.float32
BF16 = jnp.bfloat16

D_MODEL = 1024
GLA_WIDTH = 512
S5_WIDTH = 512
GLA_HEADS = 4
GLA_KEY_DIM = 256
GLA_DK = 64
GLA_DV = 128
GLA_GATE_RANK = 16
GLA_GATE_NORMALIZER = 16.0
GLA_CHUNK = 64
GLA_SPAN = 2 * GLA_CHUNK
S5_GROUP = 16
S5_GROUPS = 32
S5_STATE = 64
D_FF = 4096
EPS = 1e-6

LANES = 128
MXU_DIM = 256
S5_STATES = S5_GROUPS * S5_STATE
VMEM_LIMIT_BYTES = 56 * 1024 * 1024

GLA_BLOCK = 256
GLA_SEQS = 2
S5_BLOCK = 64
S5_CHUNK = 4
S5_PITCH = S5_BLOCK + 8
MLP_ROWS = 1024
MLP_FF_CHUNK = 1024
MLP_DOWN_ROWS = 256


def _rms_scale(x):
    return x * lax.rsqrt(jnp.mean(x * x, axis=-1, keepdims=True) + EPS)


def _rmsnorm(x, w):
    return _rms_scale(x) * w


def _bf16_parts(x, n):
    parts = []
    for _ in range(n):
        part = x.astype(BF16)
        parts.append(part)
        x = x - part.astype(F32)
    return parts


def _gla_substep(x_ref, p_new_ref, p_ref, st_ref, o_ref, first, wg_ref, wgk_ref, bgk_ref, gnw_ref,
                 tri, causal_heads, key_head, value_head, same_head, span_of_t):
    nb, L, _ = x_ref.shape
    n_spans = L // GLA_SPAN
    seqs = range(nb)
    cols = lambda i, c0, c1: p_ref[i * L:(i + 1) * L, c0:c1]

    hb = _rms_scale(x_ref[...].reshape(nb * L, D_MODEL)).astype(BF16)
    n_cols = wg_ref.shape[1]
    tiles = [(wg_ref, c0, c0) for c0 in range(0, n_cols, MXU_DIM)] + [(wgk_ref, 0, n_cols)]

    def project_tile():
        if tiles:
            w_ref, c0, dst = tiles.pop(0)
            p_new_ref[:, dst:dst + MXU_DIM] = jnp.dot(hb, w_ref[:, c0:c0 + MXU_DIM], preferred_element_type=F32)

    log_a = []
    for i in seqs:
        gk = cols(i, n_cols, n_cols + GLA_KEY_DIM) + bgk_ref[...]
        log_a.append((jnp.minimum(gk, 0.0) - jnp.log1p(jnp.exp(-jnp.abs(gk)))) * (1.0 / GLA_GATE_NORMALIZER))
    project_tile()
    project_tile()

    b = []
    for i in seqs:
        terms = jnp.dot(tri, jnp.concatenate(_bf16_parts(log_a[i], 3), axis=1), preferred_element_type=F32)
        b.append(sum(terms[:, j * GLA_KEY_DIM:(j + 1) * GLA_KEY_DIM] for j in range(3)))
    span_row = lambda t, r: jnp.concatenate(
        [jnp.broadcast_to(t[m * GLA_SPAN + r:m * GLA_SPAN + r + 1, :], (GLA_SPAN, GLA_KEY_DIM))
         for m in range(n_spans)], axis=0)
    b_mid = [span_row(b[i], GLA_CHUNK - 1) for i in seqs]
    b_last = [span_row(b[i], GLA_SPAN - 1) for i in seqs]
    project_tile()
    project_tile()

    q = [cols(i, 0, 256) * (GLA_DK ** -0.5) for i in seqs]
    q_s = [q[i] * jnp.exp(b[i] - b_mid[i]) for i in seqs]
    q_b = [(q[i] * jnp.exp(b[i])).astype(BF16) for i in seqs]
    k_et = [(cols(i, 256, 512) * jnp.exp(b_mid[i] - b[i])).T.astype(BF16) for i in seqs]
    k_dt = [(cols(i, 256, 512) * jnp.exp(b_last[i] - b[i])).T.astype(BF16) for i in seqs]
    b_t = [b[i].T for i in seqs]
    v_b = [cols(i, 512, 1024).astype(BF16) for i in seqs]
    project_tile()

    pair = MXU_DIM // GLA_DV
    q_sb = [q_s[i].astype(BF16) for i in seqs]
    o_intra = [[] for _ in seqs]
    for m in range(n_spans):
        r0, r1 = m * GLA_SPAN, (m + 1) * GLA_SPAN
        for i in seqs:
            k_span = k_et[i][:, r0:r1]
            k_blk = jnp.concatenate([jnp.where(key_head == hh, k_span, jnp.zeros_like(k_span))
                                     for hh in range(GLA_HEADS)], axis=1)
            sc = jnp.dot(q_sb[i][r0:r1], k_blk, preferred_element_type=F32)
            sc = jnp.where(causal_heads, sc, 0.0).astype(BF16)
            outs = []
            for g in range(GLA_HEADS // pair):
                v_pair = v_b[i][r0:r1, g * MXU_DIM:(g + 1) * MXU_DIM]
                v_blk = jnp.concatenate([jnp.where(value_head == j, v_pair, jnp.zeros_like(v_pair))
                                         for j in range(pair)], axis=0)
                outs.append(jnp.dot(sc[:, g * pair * GLA_SPAN:(g + 1) * pair * GLA_SPAN], v_blk,
                                    preferred_element_type=F32))
            o_intra[i].append(jnp.concatenate(outs, axis=1))
        project_tile()
        project_tile()
        project_tile()

    st = [jnp.where(first, 0.0, st_ref[i]) for i in seqs]
    o_inter = [[] for _ in seqs]
    for m in range(n_spans):
        r0, r1 = m * GLA_SPAN, (m + 1) * GLA_SPAN
        for i in seqs:
            o_inter[i].append(jnp.dot(q_b[i][r0:r1], st[i].astype(BF16), preferred_element_type=F32))
            k_m = jnp.where(span_of_t == m, k_dt[i], jnp.zeros_like(k_dt[i]))
            upd = jnp.dot(k_m, v_b[i], preferred_element_type=F32)
            decay = jnp.exp(b_t[i][:, r1 - 1:r1])
            st[i] = jnp.where(same_head, st[i] * decay + upd, 0.0)
        project_tile()
        project_tile()

    for i in seqs:
        st_ref[i] = st[i]
        o = jnp.concatenate(o_intra[i], axis=0) + jnp.concatenate(o_inter[i], axis=0)
        outs = []
        for hh in range(GLA_HEADS):
            oh = o[:, hh * GLA_DV:(hh + 1) * GLA_DV]
            outs.append(oh * lax.rsqrt(jnp.mean(oh * oh, axis=-1, keepdims=True) + EPS) * gnw_ref[...])
        g = cols(i, 1024, 1536)
        o_ref[i] = (jnp.concatenate(outs, axis=1) * (g * (1.0 / (1.0 + jnp.exp(-g))))).astype(o_ref.dtype)
    while tiles:
        project_tile()


def _gla_kernel(blocks_per_seq, xa_ref, xb_ref, wg_ref, wgl_ref, wup_ref, bgk_ref, gnw_ref, o_ref,
                st_ref, pa_ref, pb_ref, wgk_ref):
    s = pl.program_id(0)
    L = xa_ref.shape[1]

    @pl.when(s == 0)
    def _():
        st_ref[...] = jnp.zeros_like(st_ref)
        pb_ref[...] = jnp.zeros_like(pb_ref)
        wgk_ref[...] = jnp.dot(wgl_ref[...], wup_ref[...], precision=lax.Precision.HIGHEST,
                               preferred_element_type=F32).astype(BF16)

    row = lax.broadcasted_iota(jnp.int32, (L, L), 0)
    col = lax.broadcasted_iota(jnp.int32, (L, L), 1)
    causal = ((row // GLA_SPAN) == (col // GLA_SPAN)) & (col <= row)
    tri = jnp.where(causal, 1.0, 0.0).astype(BF16)
    srow = lax.broadcasted_iota(jnp.int32, st_ref.shape[1:], 0) // GLA_DK
    scol = lax.broadcasted_iota(jnp.int32, st_ref.shape[1:], 1) // GLA_DV
    same_head = srow == scol
    span_of_t = lax.broadcasted_iota(jnp.int32, (GLA_KEY_DIM, L), 1) // GLA_SPAN
    srow = lax.broadcasted_iota(jnp.int32, (GLA_SPAN, GLA_HEADS * GLA_SPAN), 0)
    scol = lax.broadcasted_iota(jnp.int32, (GLA_SPAN, GLA_HEADS * GLA_SPAN), 1) % GLA_SPAN
    causal_heads = scol <= srow
    key_head = lax.broadcasted_iota(jnp.int32, (GLA_KEY_DIM, GLA_SPAN), 0) // GLA_DK
    value_head = lax.broadcasted_iota(jnp.int32, (GLA_SPAN, MXU_DIM), 1) // GLA_DV
    shared = (wg_ref, wgk_ref, bgk_ref, gnw_ref, tri, causal_heads, key_head, value_head, same_head, span_of_t)

    _gla_substep(xa_ref, pa_ref, pb_ref, st_ref, o_ref.at[0], (2 * s - 1) % blocks_per_seq == 0, *shared)
    _gla_substep(xb_ref, pb_ref, pa_ref, st_ref, o_ref.at[1], (2 * s) % blocks_per_seq == 0, *shared)


def _gla_call(x, wg, wgl, wup, bgk, gnw):
    bsz, seq, _ = x.shape
    nb, L = GLA_SEQS, GLA_BLOCK
    bps = seq // L
    n_blocks = (bsz // nb) * bps
    assert n_blocks % 2 == 0 and bps % 2 == 0
    const = lambda shape: pl.BlockSpec(shape, lambda s: (0,) * len(shape))

    def x_spec(e):
        def index(s):
            n = jnp.minimum(2 * s + e, n_blocks - 1)
            return (n // bps, n % bps, 0)
        return pl.BlockSpec((nb, L, D_MODEL), index)

    return pl.pallas_call(
        functools.partial(_gla_kernel, bps),
        grid=(n_blocks // 2 + 1,),
        in_specs=[x_spec(0), x_spec(1), const(wg.shape), const(wgl.shape), const(wup.shape), const(bgk.shape),
                  const(gnw.shape)],
        out_specs=pl.BlockSpec((2, nb, L, GLA_WIDTH), lambda s: (s, 0, 0, 0)),
        out_shape=jax.ShapeDtypeStruct((n_blocks + 2, nb, L, GLA_WIDTH), BF16),
        scratch_shapes=[pltpu.VMEM((nb, GLA_HEADS * GLA_DK, GLA_HEADS * GLA_DV), F32),
                        pltpu.VMEM((nb * L, wg.shape[1] + GLA_KEY_DIM), F32),
                        pltpu.VMEM((nb * L, wg.shape[1] + GLA_KEY_DIM), F32),
                        pltpu.VMEM((D_MODEL, GLA_KEY_DIM), BF16)],
        compiler_params=pltpu.CompilerParams(
            dimension_semantics=("arbitrary",), vmem_limit_bytes=VMEM_LIMIT_BYTES),
        name="gla_mixer",
    )(x, x, wg, wgl, wup, bgk, gnw)


def _gelu_tanh(y):
    return 0.5 * y * (1.0 + jnp.tanh(np.float32(math.sqrt(2.0 / math.pi)) * (y + 0.044715 * (y * y * y))))


def _s5_kernel(x_ref, wu_ref, win_ref, wintra_ref, wout_ref, ajr_ref, aji_ref, wglu_ref, bglu_ref,
               o_ref, u_ref, st_ref, carry_ref, y_ref, oscr_ref):
    bsz, L, _ = x_ref.shape
    J, P = S5_CHUNK, S5_PITCH
    n_c = L // J
    n_k = S5_WIDTH // LANES
    half = LANES // 2
    hs_states = 2 * S5_STATES // (2 * n_k)
    q_pairs = hs_states // 2 // LANES
    assert J * half == MXU_DIM and J % 2 == 0
    low = lax.broadcasted_iota(jnp.int32, (1, LANES), 1) < half
    swap = lambda t: pltpu.roll(t, half, axis=1)

    @pl.when(pl.program_id(0) == 0)
    def _():
        carry_ref[...] = jnp.zeros_like(carry_ref)

    h = _rms_scale(x_ref[...].reshape(bsz * L, D_MODEL))
    u = jnp.dot(h.astype(BF16), wu_ref[...], preferred_element_type=F32)
    for bb in range(bsz):
        for k in range(n_k):
            u_ref[k, bb * P:bb * P + L, :] = u[bb * L:(bb + 1) * L, k * LANES:(k + 1) * LANES]

    for k in range(n_k):
        lhs = [[], []]
        for c in range(n_c):
            pieces = [u_ref[k, pl.ds(c * J + i, bsz, stride=P), :] for i in range(J)]
            turned = [swap(p) for p in pieces]
            lhs[0].append(jnp.concatenate(
                [jnp.where(low, pieces[i], turned[i + 1]) for i in range(0, J, 2)], axis=1))
            lhs[1].append(jnp.concatenate(
                [jnp.where(low, turned[i], pieces[i + 1]) for i in range(0, J, 2)], axis=1))

        for hh in range(2):
            hs = 2 * k + hh
            s0 = hs * hs_states
            utb = jnp.concatenate(lhs[hh], axis=0).astype(BF16)
            st_ref[:, s0:s0 + hs_states] = jnp.dot(utb, win_ref[hs], preferred_element_type=F32)
            y_ref[hs] = jnp.dot(utb, wintra_ref[hs], preferred_element_type=F32)

            for q in range(q_pairs):
                lr = s0 + q * LANES
                li = lr + hs_states // 2
                ar = jnp.broadcast_to(ajr_ref[hs:hs + 1, q * LANES:(q + 1) * LANES], (bsz, LANES))
                ai = jnp.broadcast_to(aji_ref[hs:hs + 1, q * LANES:(q + 1) * LANES], (bsz, LANES))
                xr = carry_ref[:, lr:lr + LANES]
                xi = carry_ref[:, li:li + LANES]
                for c in range(n_c):
                    rows = slice(c * bsz, (c + 1) * bsz)
                    sr = st_ref[rows, lr:lr + LANES]
                    si = st_ref[rows, li:li + LANES]
                    st_ref[rows, lr:lr + LANES] = xr
                    st_ref[rows, li:li + LANES] = xi
                    xr, xi = ar * xr - ai * xi + sr, ar * xi + ai * xr + si
                carry_ref[:, lr:lr + LANES] = xr
                carry_ref[:, li:li + LANES] = xi

            xs = st_ref[:, s0:s0 + hs_states].astype(BF16)
            y_ref[hs] = _gelu_tanh(y_ref[hs] + jnp.dot(xs, wout_ref[hs], preferred_element_type=F32))

    for r in range(J):
        v = (r // 2) * LANES
        slabs = []
        for k in range(n_k):
            ya, yb = y_ref[2 * k, :, v:v + LANES], y_ref[2 * k + 1, :, v:v + LANES]
            slabs.append(jnp.where(low, ya, swap(yb)) if r % 2 == 0 else jnp.where(low, swap(ya), yb))
        z = jnp.concatenate(slabs, axis=1)
        gate = jnp.dot(z.astype(BF16), wglu_ref[...], preferred_element_type=F32) + bglu_ref[...]
        o_r = z * (1.0 / (1.0 + jnp.exp(-gate)))
        for c in range(n_c):
            for k in range(n_k):
                oscr_ref[k, pl.ds(c * J + r, bsz, stride=P), :] = o_r[c * bsz:(c + 1) * bsz, k * LANES:(k + 1) * LANES]
    for bb in range(bsz):
        for k in range(n_k):
            o_ref[bb, :, k * LANES:(k + 1) * LANES] = oscr_ref[k, bb * P:bb * P + L, :].astype(o_ref.dtype)


def _s5_call(x, wu, win, wintra, wout, ajr, aji, wglu, bglu):
    bsz, seq, _ = x.shape
    L, J = S5_BLOCK, S5_CHUNK
    rows_c = (L // J) * bsz
    n_k = S5_WIDTH // LANES
    const = lambda shape: pl.BlockSpec(shape, lambda t: (0,) * len(shape))
    return pl.pallas_call(
        _s5_kernel,
        grid=(seq // L,),
        in_specs=[pl.BlockSpec((bsz, L, D_MODEL), lambda t: (0, t, 0))]
        + [const(a.shape) for a in (wu, win, wintra, wout, ajr, aji, wglu, bglu)],
        out_specs=pl.BlockSpec((bsz, L, S5_WIDTH), lambda t: (0, t, 0)),
        out_shape=jax.ShapeDtypeStruct((bsz, seq, S5_WIDTH), BF16),
        scratch_shapes=[
            pltpu.VMEM((n_k, bsz * S5_PITCH, LANES), F32),
            pltpu.VMEM((rows_c, 2 * S5_STATES), F32),
            pltpu.VMEM((bsz, 2 * S5_STATES), F32),
            pltpu.VMEM((2 * n_k, rows_c, J * LANES // 2), F32),
            pltpu.VMEM((n_k, bsz * S5_PITCH, LANES), F32),
        ],
        compiler_params=pltpu.CompilerParams(
            dimension_semantics=("arbitrary",), vmem_limit_bytes=VMEM_LIMIT_BYTES),
        name="s5_mixer",
    )(x, wu, win, wintra, wout, ajr, aji, wglu, bglu)


def _s5_prepare(a_re, a_im, log_dt, b_re, b_im, c_re, c_im, d_skip):
    J = S5_CHUNK
    width = LANES // 2
    n_k = S5_WIDTH // width
    gpk = S5_GROUPS // n_k
    lanes = gpk * S5_STATE
    hi = lax.Precision.HIGHEST

    per_state = lambda t: t.reshape(n_k, 1, lanes)
    ar, ai = per_state(a_re), per_state(a_im)
    dt = per_state(jnp.broadcast_to(jnp.exp(log_dt)[:, None], a_re.shape))
    b_t = lambda t: t.reshape(n_k, gpk, S5_STATE, S5_GROUP).transpose(0, 3, 1, 2).reshape(n_k, S5_GROUP, lanes)
    c_t = lambda t: t.reshape(n_k, gpk, S5_GROUP, S5_STATE).transpose(0, 2, 1, 3).reshape(n_k, S5_GROUP, lanes)
    b_re, b_im, c_re, c_im = b_t(b_re), b_t(b_im), c_t(c_re), c_t(c_im)

    def apow(m):
        mag = jnp.exp(m * ar * dt)
        return mag * jnp.cos(m * ai * dt), mag * jnp.sin(m * ai * dt)

    abr, abi = apow(1)
    den = ar * ar + ai * ai
    nr = abr - 1.0
    fr = (nr * ar + abi * ai) / den
    fi = (abi * ar - nr * ai) / den
    bbr = fr * b_re - fi * b_im
    bbi = fr * b_im + fi * b_re

    ab_re, ab_im = [], []
    for m in range(J):
        pr, pi = apow(m)
        ab_re.append(pr * bbr - pi * bbi)
        ab_im.append(pr * bbi + pi * bbr)

    def block_diag(t, lane_period, lane_group_width):
        n_lanes = t.shape[-1]
        row_g = lax.broadcasted_iota(jnp.int32, (gpk, 1, n_lanes), 0)
        lane_g = (lax.broadcasted_iota(jnp.int32, (gpk, 1, n_lanes), 2) % lane_period) // lane_group_width
        full = jnp.where(row_g == lane_g, t[:, :, None, :, :], 0.0)
        return full.reshape(n_k, J * width, n_lanes)

    win = block_diag(jnp.stack(
        [jnp.concatenate([ab_re[J - 1 - i], ab_im[J - 1 - i]], axis=-1) for i in range(J)], axis=1),
        lanes, S5_STATE)

    t_out = []
    for r in range(J):
        pr, pi = apow(r + 1)
        t_out.append(jnp.concatenate([c_re * pr - c_im * pi, -(c_re * pi + c_im * pr)], axis=-1))
    wout = block_diag(jnp.stack(t_out, axis=1), lanes, S5_STATE).transpose(0, 2, 1)

    grp = lambda t: t.reshape(n_k, S5_GROUP, gpk, S5_STATE)
    kmat = [(jnp.einsum('kqgn,kpgn->kpgq', grp(c_re), grp(ab_re[m]), precision=hi)
             - jnp.einsum('kqgn,kpgn->kpgq', grp(c_im), grp(ab_im[m]), precision=hi)).reshape(n_k, S5_GROUP, width)
            for m in range(J)]
    d_t = d_skip.reshape(n_k, gpk, S5_GROUP).transpose(0, 2, 1)
    p_eq_q = (lax.broadcasted_iota(jnp.int32, (S5_GROUP, gpk, S5_GROUP), 0)
              == lax.broadcasted_iota(jnp.int32, (S5_GROUP, gpk, S5_GROUP), 2))
    kmat[0] = kmat[0] + jnp.where(p_eq_q, d_t[..., None], 0.0).reshape(n_k, S5_GROUP, width)
    zero = jnp.zeros_like(kmat[0])
    t_intra = jnp.stack([jnp.concatenate([kmat[r - i] if i <= r else zero for r in range(J)], axis=-1)
                         for i in range(J)], axis=1)
    wintra = block_diag(t_intra, width, S5_GROUP)

    ajr, aji = apow(J)
    return (win.astype(BF16), wintra.astype(BF16), wout.astype(BF16), ajr.reshape(n_k, lanes), aji.reshape(n_k, lanes))


def _out_mlp_kernel(n_og, x_ref, *refs):
    og_refs = refs[:n_og]
    os_ref, wo_ref, wup_ref, wdn_ref, fw_ref, o_ref = refs[n_og:]
    L = og_refs[0].shape[0]
    x1_groups = [x_ref[j * L:(j + 1) * L, :]
                 + jnp.dot(og_ref[...], wo_ref[0:GLA_WIDTH, :], preferred_element_type=F32)
                 + jnp.dot(os_ref[j * L:(j + 1) * L, :], wo_ref[GLA_WIDTH:, :], preferred_element_type=F32)
                 for j, og_ref in enumerate(og_refs)]
    h = jnp.concatenate([_rms_scale(g).astype(BF16) for g in x1_groups], axis=0)
    x1 = jnp.concatenate(x1_groups, axis=0)
    acts = []
    for c in range(D_FF // MLP_FF_CHUNK):
        cs = slice(c * MLP_FF_CHUNK, (c + 1) * MLP_FF_CHUNK)
        a = jnp.maximum(jnp.dot(h, wup_ref[:, cs], preferred_element_type=F32), 0.0)
        acts.append((a * a).astype(BF16))
    act = jnp.concatenate(acts, axis=1)
    for r0 in range(0, MLP_ROWS, MLP_DOWN_ROWS):
        r = slice(r0, r0 + MLP_DOWN_ROWS)
        x2 = x1[r] + jnp.dot(act[r], wdn_ref[...], preferred_element_type=F32)
        o_ref[r, :] = _rmsnorm(x2, fw_ref[...])


def _out_mlp_call(x2, og_blocks, os_, wo, wup, wdn, fw, seq):
    rows = x2.shape[0]
    nb, L = og_blocks.shape[1], og_blocks.shape[2]
    assert MLP_ROWS % L == 0 and seq % MLP_ROWS == 0
    n_og = MLP_ROWS // L
    tiles_per_seq = seq // MLP_ROWS
    const = lambda shape: pl.BlockSpec(shape, lambda i: (0,) * len(shape), pipeline_mode=pl.Buffered(1))
    tile = lambda width: pl.BlockSpec((MLP_ROWS, width), lambda i: (i, 0))

    def og_spec(e):
        def index(i):
            b, t = i // tiles_per_seq, i % tiles_per_seq
            return ((b // nb) * (seq // L) + n_og * t + e + 1, b % nb, 0, 0)
        return pl.BlockSpec((None, None, L, GLA_WIDTH), index)

    return pl.pallas_call(
        functools.partial(_out_mlp_kernel, n_og),
        grid=(rows // MLP_ROWS,),
        in_specs=[tile(D_MODEL)] + [og_spec(e) for e in range(n_og)] + [tile(S5_WIDTH)]
        + [const(a.shape) for a in (wo, wup, wdn, fw)],
        out_specs=tile(D_MODEL),
        out_shape=jax.ShapeDtypeStruct((rows, D_MODEL), F32),
        compiler_params=pltpu.CompilerParams(
            dimension_semantics=("arbitrary",), vmem_limit_bytes=VMEM_LIMIT_BYTES),
        name="out_mlp",
    )(x2, *([og_blocks] * n_og), os_, wo, wup, wdn, fw)


def kernel(x, norm_mix_w, w_in, w_gk_up, b_gk, gla_norm_w, s5_a_re, s5_a_im, s5_log_dt, s5_b_re, s5_b_im,
           s5_c_re, s5_c_im, s5_d, w_glu, b_glu, w_out, norm_mlp_w, w_mlp_up, w_mlp_down, norm_final_w):
    bsz, seq, _ = x.shape
    assert bsz % GLA_SEQS == 0 and seq % GLA_BLOCK == 0 and seq % S5_BLOCK == 0 and (bsz * seq) % MLP_ROWS == 0
    assert norm_mix_w.shape[0] == 1, "single-layer problem"
    for l in range(1):
        gain = norm_mix_w[l][:, None]
        w = (w_in[l] * gain).astype(BF16)
        kd2, r = 2 * GLA_KEY_DIM, GLA_GATE_RANK
        wg = jnp.concatenate([w[:, 0:kd2 + GLA_WIDTH], w[:, kd2 + GLA_WIDTH + r:kd2 + 2 * GLA_WIDTH + r]], axis=1)
        wgl = jnp.concatenate([w_in[l][:, kd2 + GLA_WIDTH:kd2 + GLA_WIDTH + r] * gain,
                               jnp.zeros((D_MODEL, LANES - r), F32)], axis=1)
        wu = w[:, kd2 + 2 * GLA_WIDTH + r:]
        wup = jnp.concatenate([w_gk_up[l], jnp.zeros((LANES - r, GLA_KEY_DIM), F32)], axis=0)

        o_gla = _gla_call(x, wg, wgl, wup, b_gk[l][None, :], gla_norm_w[l][None, :])

        win, wintra, wout, ajr, aji = _s5_prepare(s5_a_re[l], s5_a_im[l], s5_log_dt[l], s5_b_re[l], s5_b_im[l],
                                                  s5_c_re[l], s5_c_im[l], s5_d[l])
        o_s5 = _s5_call(x, wu, win, wintra, wout, ajr, aji, w_glu[l].astype(BF16), b_glu[l][None, :])

        rows = bsz * seq
        out = _out_mlp_call(x.reshape(rows, D_MODEL), o_gla, o_s5.reshape(rows, S5_WIDTH),
                            w_out[l].astype(BF16), (w_mlp_up[l] * norm_mlp_w[l][:, None]).astype(BF16),
                            w_mlp_down[l].astype(BF16), norm_final_w[None, :], seq)
        x = out.reshape(bsz, seq, D_MODEL)
    return x
```

```python
import functools
import math

import jax
import jax.numpy as jnp
import numpy as np
from jax import lax
from jax.experimental import pallas as pl
from jax.experimental.pallas import tpu as pltpu

F32 = jnp.float32
BF16 = jnp.bfloat16

D_MODEL = 1024
GLA_WIDTH = 512
S5_WIDTH = 512
GLA_HEADS = 4
GLA_KEY_DIM = 256
GLA_DK = 64
GLA_DV = 128
GLA_GATE_RANK = 16
GLA_GATE_NORMALIZER = 16.0
GLA_CHUNK = 64
GLA_SPAN = 2 * GLA_CHUNK
S5_GROUP = 16
S5_GROUPS = 32
S5_STATE = 64
D_FF = 4096
EPS = 1e-6

LANES = 128
MXU_DIM = 256
S5_STATES = S5_GROUPS * S5_STATE
VMEM_LIMIT_BYTES = 56 * 1024 * 1024

GLA_BLOCK = 256
GLA_SEQS = 2
S5_BLOCK = 64
S5_CHUNK = 4
S5_PITCH = S5_BLOCK + 8
MLP_ROWS = 1024
MLP_FF_CHUNK = 1024
MLP_DOWN_ROWS = 256


def _rms_scale(x):
    return x * lax.rsqrt(jnp.mean(x * x, axis=-1, keepdims=True) + EPS)


def _rmsnorm(x, w):
    return _rms_scale(x) * w


def _bf16_parts(x, n):
    parts = []
    for _ in range(n):
        part = x.astype(BF16)
        parts.append(part)
        x = x - part.astype(F32)
    return parts


def _gla_substep(x_ref, p_new_ref, p_ref, st_ref, o_ref, first, wg_ref, wgk_ref, bgk_ref, gnw_ref,
                 tri, causal_heads, key_head, value_head, same_head, span_of_t):
    nb, L, _ = x_ref.shape
    n_spans = L // GLA_SPAN
    seqs = range(nb)
    cols = lambda i, c0, c1: p_ref[i * L:(i + 1) * L, c0:c1]

    hb = _rms_scale(x_ref[...].reshape(nb * L, D_MODEL)).astype(BF16)
    n_cols = wg_ref.shape[1]
    tiles = [(wg_ref, c0, c0) for c0 in range(0, n_cols, MXU_DIM)] + [(wgk_ref, 0, n_cols)]

    def project_tile():
        if tiles:
            w_ref, c0, dst = tiles.pop(0)
            p_new_ref[:, dst:dst + MXU_DIM] = jnp.dot(hb, w_ref[:, c0:c0 + MXU_DIM], preferred_element_type=F32)

    log_a = []
    for i in seqs:
        gk = cols(i, n_cols, n_cols + GLA_KEY_DIM) + bgk_ref[...]
        log_a.append((jnp.minimum(gk, 0.0) - jnp.log1p(jnp.exp(-jnp.abs(gk)))) * (1.0 / GLA_GATE_NORMALIZER))
    project_tile()
    project_tile()

    b = []
    for i in seqs:
        terms = jnp.dot(tri, jnp.concatenate(_bf16_parts(log_a[i], 3), axis=1), preferred_element_type=F32)
        b.append(sum(terms[:, j * GLA_KEY_DIM:(j + 1) * GLA_KEY_DIM] for j in range(3)))
    span_row = lambda t, r: jnp.concatenate(
        [jnp.broadcast_to(t[m * GLA_SPAN + r:m * GLA_SPAN + r + 1, :], (GLA_SPAN, GLA_KEY_DIM))
         for m in range(n_spans)], axis=0)
    b_mid = [span_row(b[i], GLA_CHUNK - 1) for i in seqs]
    b_last = [span_row(b[i], GLA_SPAN - 1) for i in seqs]
    project_tile()
    project_tile()

    q = [cols(i, 0, 256) * (GLA_DK ** -0.5) for i in seqs]
    q_s = [q[i] * jnp.exp(b[i] - b_mid[i]) for i in seqs]
    q_b = [(q[i] * jnp.exp(b[i])).astype(BF16) for i in seqs]
    k_et = [(cols(i, 256, 512) * jnp.exp(b_mid[i] - b[i])).T.astype(BF16) for i in seqs]
    k_dt = [(cols(i, 256, 512) * jnp.exp(b_last[i] - b[i])).T.astype(BF16) for i in seqs]
    b_t = [b[i].T for i in seqs]
    v_b = [cols(i, 512, 1024).astype(BF16) for i in seqs]
    project_tile()

    pair = MXU_DIM // GLA_DV
    q_sb = [q_s[i].astype(BF16) for i in seqs]
    o_intra = [[] for _ in seqs]
    for m in range(n_spans):
        r0, r1 = m * GLA_SPAN, (m + 1) * GLA_SPAN
        for i in seqs:
            k_span = k_et[i][:, r0:r1]
            k_blk = jnp.concatenate([jnp.where(key_head == hh, k_span, jnp.zeros_like(k_span))
                                     for hh in range(GLA_HEADS)], axis=1)
            sc = jnp.dot(q_sb[i][r0:r1], k_blk, preferred_element_type=F32)
            sc = jnp.where(causal_heads, sc, 0.0).astype(BF16)
            outs = []
            for g in range(GLA_HEADS // pair):
                v_pair = v_b[i][r0:r1, g * MXU_DIM:(g + 1) * MXU_DIM]
                v_blk = jnp.concatenate([jnp.where(value_head == j, v_pair, jnp.zeros_like(v_pair))
                                         for j in range(pair)], axis=0)
                outs.append(jnp.dot(sc[:, g * pair * GLA_SPAN:(g + 1) * pair * GLA_SPAN], v_blk,
                                    preferred_element_type=F32))
            o_intra[i].append(jnp.concatenate(outs, axis=1))
        project_tile()
        project_tile()
        project_tile()

    n_pairs = GLA_HEADS // pair
    kp, vp = pair * GLA_DK, pair * GLA_DV
    st = [[jnp.where(first, 0.0, st_ref[i, g]) for g in range(n_pairs)] for i in seqs]
    o_inter = [[] for _ in seqs]
    for m in range(n_spans):
        r0, r1 = m * GLA_SPAN, (m + 1) * GLA_SPAN
        for i in seqs:
            k_m = jnp.where(span_of_t == m, k_dt[i], jnp.zeros_like(k_dt[i]))
            decay = jnp.exp(b_t[i][:, r1 - 1:r1])
            outs = []
            for g in range(n_pairs):
                outs.append(jnp.dot(q_b[i][r0:r1, g * kp:(g + 1) * kp], st[i][g].astype(BF16),
                                    preferred_element_type=F32))
                upd = jnp.dot(k_m[g * kp:(g + 1) * kp], v_b[i][:, g * vp:(g + 1) * vp],
                              preferred_element_type=F32)
                st[i][g] = jnp.where(same_head, st[i][g] * decay[g * kp:(g + 1) * kp] + upd, 0.0)
            o_inter[i].append(jnp.concatenate(outs, axis=1))
        project_tile()
        project_tile()

    for i in seqs:
        for g in range(n_pairs):
            st_ref[i, g] = st[i][g]
        o = jnp.concatenate(o_intra[i], axis=0) + jnp.concatenate(o_inter[i], axis=0)
        outs = []
        for hh in range(GLA_HEADS):
            oh = o[:, hh * GLA_DV:(hh + 1) * GLA_DV]
            outs.append(oh * lax.rsqrt(jnp.mean(oh * oh, axis=-1, keepdims=True) + EPS) * gnw_ref[...])
        g = cols(i, 1024, 1536)
        o_ref[i] = (jnp.concatenate(outs, axis=1) * (g * (1.0 / (1.0 + jnp.exp(-g))))).astype(o_ref.dtype)
    while tiles:
        project_tile()


def _gla_kernel(blocks_per_seq, xa_ref, xb_ref, wg_ref, wgl_ref, wup_ref, bgk_ref, gnw_ref, o_ref,
                st_ref, pa_ref, pb_ref, wgk_ref):
    s = pl.program_id(0)
    L = xa_ref.shape[1]

    @pl.when(s == 0)
    def _():
        st_ref[...] = jnp.zeros_like(st_ref)
        pb_ref[...] = jnp.zeros_like(pb_ref)
        wgk_ref[...] = jnp.dot(wgl_ref[...], wup_ref[...], precision=lax.Precision.HIGHEST,
                               preferred_element_type=F32).astype(BF16)

    row = lax.broadcasted_iota(jnp.int32, (L, L), 0)
    col = lax.broadcasted_iota(jnp.int32, (L, L), 1)
    causal = ((row // GLA_SPAN) == (col // GLA_SPAN)) & (col <= row)
    tri = jnp.where(causal, 1.0, 0.0).astype(BF16)
    srow = lax.broadcasted_iota(jnp.int32, st_ref.shape[2:], 0) // GLA_DK
    scol = lax.broadcasted_iota(jnp.int32, st_ref.shape[2:], 1) // GLA_DV
    same_head = srow == scol
    span_of_t = lax.broadcasted_iota(jnp.int32, (GLA_KEY_DIM, L), 1) // GLA_SPAN
    srow = lax.broadcasted_iota(jnp.int32, (GLA_SPAN, GLA_HEADS * GLA_SPAN), 0)
    scol = lax.broadcasted_iota(jnp.int32, (GLA_SPAN, GLA_HEADS * GLA_SPAN), 1) % GLA_SPAN
    causal_heads = scol <= srow
    key_head = lax.broadcasted_iota(jnp.int32, (GLA_KEY_DIM, GLA_SPAN), 0) // GLA_DK
    value_head = lax.broadcasted_iota(jnp.int32, (GLA_SPAN, MXU_DIM), 1) // GLA_DV
    shared = (wg_ref, wgk_ref, bgk_ref, gnw_ref, tri, causal_heads, key_head, value_head, same_head, span_of_t)

    _gla_substep(xa_ref, pa_ref, pb_ref, st_ref, o_ref.at[0], (2 * s - 1) % blocks_per_seq == 0, *shared)
    _gla_substep(xb_ref, pb_ref, pa_ref, st_ref, o_ref.at[1], (2 * s) % blocks_per_seq == 0, *shared)


def _gla_call(x, wg, wgl, wup, bgk, gnw):
    bsz, seq, _ = x.shape
    nb, L = GLA_SEQS, GLA_BLOCK
    bps = seq // L
    n_blocks = (bsz // nb) * bps
    assert n_blocks % 2 == 0 and bps % 2 == 0
    const = lambda shape: pl.BlockSpec(shape, lambda s: (0,) * len(shape))

    def x_spec(e):
        def index(s):
            n = jnp.minimum(2 * s + e, n_blocks - 1)
            return (n // bps, n % bps, 0)
        return pl.BlockSpec((nb, L, D_MODEL), index)

    return pl.pallas_call(
        functools.partial(_gla_kernel, bps),
        grid=(n_blocks // 2 + 1,),
        in_specs=[x_spec(0), x_spec(1), const(wg.shape), const(wgl.shape), const(wup.shape), const(bgk.shape),
                  const(gnw.shape)],
        out_specs=pl.BlockSpec((2, nb, L, GLA_WIDTH), lambda s: (s, 0, 0, 0)),
        out_shape=jax.ShapeDtypeStruct((n_blocks + 2, nb, L, GLA_WIDTH), BF16),
        scratch_shapes=[pltpu.VMEM((nb, GLA_HEADS * GLA_DV // MXU_DIM, MXU_DIM // GLA_DV * GLA_DK, MXU_DIM), F32),
                        pltpu.VMEM((nb * L, wg.shape[1] + GLA_KEY_DIM), F32),
                        pltpu.VMEM((nb * L, wg.shape[1] + GLA_KEY_DIM), F32),
                        pltpu.VMEM((D_MODEL, GLA_KEY_DIM), BF16)],
        compiler_params=pltpu.CompilerParams(
            dimension_semantics=("arbitrary",), vmem_limit_bytes=VMEM_LIMIT_BYTES),
        name="gla_mixer",
    )(x, x, wg, wgl, wup, bgk, gnw)


def _gelu_tanh(y):
    return 0.5 * y * (1.0 + jnp.tanh(np.float32(math.sqrt(2.0 / math.pi)) * (y + 0.044715 * (y * y * y))))


def _s5_kernel(x_ref, wu_ref, win_ref, wintra_ref, wout_ref, ajr_ref, aji_ref, wglu_ref, bglu_ref,
               o_ref, u_ref, st_ref, carry_ref, y_ref, oscr_ref):
    bsz, L, _ = x_ref.shape
    J, P = S5_CHUNK, S5_PITCH
    n_c = L // J
    n_k = S5_WIDTH // LANES
    half = LANES // 2
    hs_states = 2 * S5_STATES // (2 * n_k)
    q_pairs = hs_states // 2 // LANES
    assert J * half == MXU_DIM and J % 2 == 0
    low = lax.broadcasted_iota(jnp.int32, (1, LANES), 1) < half
    swap = lambda t: pltpu.roll(t, half, axis=1)

    @pl.when(pl.program_id(0) == 0)
    def _():
        carry_ref[...] = jnp.zeros_like(carry_ref)

    h = _rms_scale(x_ref[...].reshape(bsz * L, D_MODEL))
    u = jnp.dot(h.astype(BF16), wu_ref[...], preferred_element_type=F32)
    for bb in range(bsz):
        for k in range(n_k):
            u_ref[k, bb * P:bb * P + L, :] = u[bb * L:(bb + 1) * L, k * LANES:(k + 1) * LANES]

    for k in range(n_k):
        lhs = [[], []]
        for c in range(n_c):
            pieces = [u_ref[k, pl.ds(c * J + i, bsz, stride=P), :] for i in range(J)]
            turned = [swap(p) for p in pieces]
            lhs[0].append(jnp.concatenate(
                [jnp.where(low, pieces[i], turned[i + 1]) for i in range(0, J, 2)], axis=1))
            lhs[1].append(jnp.concatenate(
                [jnp.where(low, turned[i], pieces[i + 1]) for i in range(0, J, 2)], axis=1))

        for hh in range(2):
            hs = 2 * k + hh
            s0 = hs * hs_states
            utb = jnp.concatenate(lhs[hh], axis=0).astype(BF16)
            st_ref[:, s0:s0 + hs_states] = jnp.dot(utb, win_ref[hs], preferred_element_type=F32)
            y_ref[hs] = jnp.dot(utb, wintra_ref[hs], preferred_element_type=F32)

            for q in range(q_pairs):
                lr = s0 + q * LANES
                li = lr + hs_states // 2
                ar = jnp.broadcast_to(ajr_ref[hs:hs + 1, q * LANES:(q + 1) * LANES], (bsz, LANES))
                ai = jnp.broadcast_to(aji_ref[hs:hs + 1, q * LANES:(q + 1) * LANES], (bsz, LANES))
                xr = carry_ref[:, lr:lr + LANES]
                xi = carry_ref[:, li:li + LANES]
                for c in range(n_c):
                    rows = slice(c * bsz, (c + 1) * bsz)
                    sr = st_ref[rows, lr:lr + LANES]
                    si = st_ref[rows, li:li + LANES]
                    st_ref[rows, lr:lr + LANES] = xr
                    st_ref[rows, li:li + LANES] = xi
                    xr, xi = ar * xr - ai * xi + sr, ar * xi + ai * xr + si
                carry_ref[:, lr:lr + LANES] = xr
                carry_ref[:, li:li + LANES] = xi

            xs = st_ref[:, s0:s0 + hs_states].astype(BF16)
            y_ref[hs] = _gelu_tanh(y_ref[hs] + jnp.dot(xs, wout_ref[hs], preferred_element_type=F32))

    for r in range(J):
        v = (r // 2) * LANES
        slabs = []
        for k in range(n_k):
            ya, yb = y_ref[2 * k, :, v:v + LANES], y_ref[2 * k + 1, :, v:v + LANES]
            slabs.append(jnp.where(low, ya, swap(yb)) if r % 2 == 0 else jnp.where(low, swap(ya), yb))
        z = jnp.concatenate(slabs, axis=1)
        gate = jnp.dot(z.astype(BF16), wglu_ref[...], preferred_element_type=F32) + bglu_ref[...]
        o_r = z * (1.0 / (1.0 + jnp.exp(-gate)))
        for c in range(n_c):
            for k in range(n_k):
                oscr_ref[k, pl.ds(c * J + r, bsz, stride=P), :] = o_r[c * bsz:(c + 1) * bsz, k * LANES:(k + 1) * LANES]
    for bb in range(bsz):
        for k in range(n_k):
            o_ref[bb, :, k * LANES:(k + 1) * LANES] = oscr_ref[k, bb * P:bb * P + L, :].astype(o_ref.dtype)


def _s5_call(x, wu, win, wintra, wout, ajr, aji, wglu, bglu):
    bsz, seq, _ = x.shape
    L, J = S5_BLOCK, S5_CHUNK
    rows_c = (L // J) * bsz
    n_k = S5_WIDTH // LANES
    const = lambda shape: pl.BlockSpec(shape, lambda t: (0,) * len(shape))
    return pl.pallas_call(
        _s5_kernel,
        grid=(seq // L,),
        in_specs=[pl.BlockSpec((bsz, L, D_MODEL), lambda t: (0, t, 0))]
        + [const(a.shape) for a in (wu, win, wintra, wout, ajr, aji, wglu, bglu)],
        out_specs=pl.BlockSpec((bsz, L, S5_WIDTH), lambda t: (0, t, 0)),
        out_shape=jax.ShapeDtypeStruct((bsz, seq, S5_WIDTH), BF16),
        scratch_shapes=[
            pltpu.VMEM((n_k, bsz * S5_PITCH, LANES), F32),
            pltpu.VMEM((rows_c, 2 * S5_STATES), F32),
            pltpu.VMEM((bsz, 2 * S5_STATES), F32),
            pltpu.VMEM((2 * n_k, rows_c, J * LANES // 2), F32),
            pltpu.VMEM((n_k, bsz * S5_PITCH, LANES), F32),
        ],
        compiler_params=pltpu.CompilerParams(
            dimension_semantics=("arbitrary",), vmem_limit_bytes=VMEM_LIMIT_BYTES),
        name="s5_mixer",
    )(x, wu, win, wintra, wout, ajr, aji, wglu, bglu)


def _s5_prepare(a_re, a_im, log_dt, b_re, b_im, c_re, c_im, d_skip):
    J = S5_CHUNK
    width = LANES // 2
    n_k = S5_WIDTH // width
    gpk = S5_GROUPS // n_k
    lanes = gpk * S5_STATE
    hi = lax.Precision.HIGHEST

    per_state = lambda t: t.reshape(n_k, 1, lanes)
    ar, ai = per_state(a_re), per_state(a_im)
    dt = per_state(jnp.broadcast_to(jnp.exp(log_dt)[:, None], a_re.shape))
    b_t = lambda t: t.reshape(n_k, gpk, S5_STATE, S5_GROUP).transpose(0, 3, 1, 2).reshape(n_k, S5_GROUP, lanes)
    c_t = lambda t: t.reshape(n_k, gpk, S5_GROUP, S5_STATE).transpose(0, 2, 1, 3).reshape(n_k, S5_GROUP, lanes)
    b_re, b_im, c_re, c_im = b_t(b_re), b_t(b_im), c_t(c_re), c_t(c_im)

    def apow(m):
        mag = jnp.exp(m * ar * dt)
        return mag * jnp.cos(m * ai * dt), mag * jnp.sin(m * ai * dt)

    abr, abi = apow(1)
    den = ar * ar + ai * ai
    nr = abr - 1.0
    fr = (nr * ar + abi * ai) / den
    fi = (abi * ar - nr * ai) / den
    bbr = fr * b_re - fi * b_im
    bbi = fr * b_im + fi * b_re

    ab_re, ab_im = [], []
    for m in range(J):
        pr, pi = apow(m)
        ab_re.append(pr * bbr - pi * bbi)
        ab_im.append(pr * bbi + pi * bbr)

    def block_diag(t, lane_period, lane_group_width):
        n_lanes = t.shape[-1]
        row_g = lax.broadcasted_iota(jnp.int32, (gpk, 1, n_lanes), 0)
        lane_g = (lax.broadcasted_iota(jnp.int32, (gpk, 1, n_lanes), 2) % lane_period) // lane_group_width
        full = jnp.where(row_g == lane_g, t[:, :, None, :, :], 0.0)
        return full.reshape(n_k, J * width, n_lanes)

    win = block_diag(jnp.stack(
        [jnp.concatenate([ab_re[J - 1 - i], ab_im[J - 1 - i]], axis=-1) for i in range(J)], axis=1),
        lanes, S5_STATE)

    t_out = []
    for r in range(J):
        pr, pi = apow(r + 1)
        t_out.append(jnp.concatenate([c_re * pr - c_im * pi, -(c_re * pi + c_im * pr)], axis=-1))
    wout = block_diag(jnp.stack(t_out, axis=1), lanes, S5_STATE).transpose(0, 2, 1)

    grp = lambda t: t.reshape(n_k, S5_GROUP, gpk, S5_STATE)
    kmat = [(jnp.einsum('kqgn,kpgn->kpgq', grp(c_re), grp(ab_re[m]), precision=hi)
             - jnp.einsum('kqgn,kpgn->kpgq', grp(c_im), grp(ab_im[m]), precision=hi)).reshape(n_k, S5_GROUP, width)
            for m in range(J)]
    d_t = d_skip.reshape(n_k, gpk, S5_GROUP).transpose(0, 2, 1)
    p_eq_q = (lax.broadcasted_iota(jnp.int32, (S5_GROUP, gpk, S5_GROUP), 0)
              == lax.broadcasted_iota(jnp.int32, (S5_GROUP, gpk, S5_GROUP), 2))
    kmat[0] = kmat[0] + jnp.where(p_eq_q, d_t[..., None], 0.0).reshape(n_k, S5_GROUP, width)
    zero = jnp.zeros_like(kmat[0])
    t_intra = jnp.stack([jnp.concatenate([kmat[r - i] if i <= r else zero for r in range(J)], axis=-1)
                         for i in range(J)], axis=1)
    wintra = block_diag(t_intra, width, S5_GROUP)

    ajr, aji = apow(J)
    return (win.astype(BF16), wintra.astype(BF16), wout.astype(BF16), ajr.reshape(n_k, lanes), aji.reshape(n_k, lanes))


def _out_mlp_kernel(n_og, x_ref, *refs):
    og_refs = refs[:n_og]
    os_ref, wo_ref, wup_ref, wdn_ref, fw_ref, o_ref = refs[n_og:]
    L = og_refs[0].shape[0]
    x1_groups = [x_ref[j * L:(j + 1) * L, :]
                 + jnp.dot(og_ref[...], wo_ref[0:GLA_WIDTH, :], preferred_element_type=F32)
                 + jnp.dot(os_ref[j * L:(j + 1) * L, :], wo_ref[GLA_WIDTH:, :], preferred_element_type=F32)
                 for j, og_ref in enumerate(og_refs)]
    h = jnp.concatenate([_rms_scale(g).astype(BF16) for g in x1_groups], axis=0)
    x1 = jnp.concatenate(x1_groups, axis=0)
    acts = []
    for c in range(D_FF // MLP_FF_CHUNK):
        cs = slice(c * MLP_FF_CHUNK, (c + 1) * MLP_FF_CHUNK)
        a = jnp.maximum(jnp.dot(h, wup_ref[:, cs], preferred_element_type=F32), 0.0)
        acts.append((a * a).astype(BF16))
    act = jnp.concatenate(acts, axis=1)
    for r0 in range(0, MLP_ROWS, MLP_DOWN_ROWS):
        r = slice(r0, r0 + MLP_DOWN_ROWS)
        x2 = x1[r] + jnp.dot(act[r], wdn_ref[...], preferred_element_type=F32)
        o_ref[r, :] = _rmsnorm(x2, fw_ref[...])


def _out_mlp_call(x2, og_blocks, os_, wo, wup, wdn, fw, seq):
    rows = x2.shape[0]
    nb, L = og_blocks.shape[1], og_blocks.shape[2]
    assert MLP_ROWS % L == 0 and seq % MLP_ROWS == 0
    n_og = MLP_ROWS // L
    tiles_per_seq = seq // MLP_ROWS
    const = lambda shape: pl.BlockSpec(shape, lambda i: (0,) * len(shape), pipeline_mode=pl.Buffered(1))
    tile = lambda width: pl.BlockSpec((MLP_ROWS, width), lambda i: (i, 0))

    def og_spec(e):
        def index(i):
            b, t = i // tiles_per_seq, i % tiles_per_seq
            return ((b // nb) * (seq // L) + n_og * t + e + 1, b % nb, 0, 0)
        return pl.BlockSpec((None, None, L, GLA_WIDTH), index)

    return pl.pallas_call(
        functools.partial(_out_mlp_kernel, n_og),
        grid=(rows // MLP_ROWS,),
        in_specs=[tile(D_MODEL)] + [og_spec(e) for e in range(n_og)] + [tile(S5_WIDTH)]
        + [const(a.shape) for a in (wo, wup, wdn, fw)],
        out_specs=tile(D_MODEL),
        out_shape=jax.ShapeDtypeStruct((rows, D_MODEL), F32),
        compiler_params=pltpu.CompilerParams(
            dimension_semantics=("arbitrary",), vmem_limit_bytes=VMEM_LIMIT_BYTES),
        name="out_mlp",
    )(x2, *([og_blocks] * n_og), os_, wo, wup, wdn, fw)


def kernel(x, norm_mix_w, w_in, w_gk_up, b_gk, gla_norm_w, s5_a_re, s5_a_im, s5_log_dt, s5_b_re, s5_b_im,
           s5_c_re, s5_c_im, s5_d, w_glu, b_glu, w_out, norm_mlp_w, w_mlp_up, w_mlp_down, norm_final_w):
    bsz, seq, _ = x.shape
    assert bsz % GLA_SEQS == 0 and seq % GLA_BLOCK == 0 and seq % S5_BLOCK == 0 and (bsz * seq) % MLP_ROWS == 0
    assert norm_mix_w.shape[0] == 1, "single-layer problem"
    for l in range(1):
        gain = norm_mix_w[l][:, None]
        w = (w_in[l] * gain).astype(BF16)
        kd2, r = 2 * GLA_KEY_DIM, GLA_GATE_RANK
        wg = jnp.concatenate([w[:, 0:kd2 + GLA_WIDTH], w[:, kd2 + GLA_WIDTH + r:kd2 + 2 * GLA_WIDTH + r]], axis=1)
        wgl = jnp.concatenate([w_in[l][:, kd2 + GLA_WIDTH:kd2 + GLA_WIDTH + r] * gain,
                               jnp.zeros((D_MODEL, LANES - r), F32)], axis=1)
        wu = w[:, kd2 + 2 * GLA_WIDTH + r:]
        wup = jnp.concatenate([w_gk_up[l], jnp.zeros((LANES - r, GLA_KEY_DIM), F32)], axis=0)

        o_gla = _gla_call(x, wg, wgl, wup, b_gk[l][None, :], gla_norm_w[l][None, :])

        win, wintra, wout, ajr, aji = _s5_prepare(s5_a_re[l], s5_a_im[l], s5_log_dt[l], s5_b_re[l], s5_b_im[l],
                                                  s5_c_re[l], s5_c_im[l], s5_d[l])
        o_s5 = _s5_call(x, wu, win, wintra, wout, ajr, aji, w_glu[l].astype(BF16), b_glu[l][None, :])

        rows = bsz * seq
        out = _out_mlp_call(x.reshape(rows, D_MODEL), o_gla, o_s5.reshape(rows, S5_WIDTH),
                            w_out[l].astype(BF16), (w_mlp_up[l] * norm_mlp_w[l][:, None]).astype(BF16),
                            w_mlp_down[l].astype(BF16), norm_final_w[None, :], seq)
        x = out.reshape(bsz, seq, D_MODEL)
    return x
```
